```python
import jax, jax.numpy as jnp
from jax import lax
import numpy as np

D_MODEL = 1024
BATCH = 16
SEQ = 256
DEPTH = 2
DEC_BATCH = 2
DEC_SEQ = 2048
PAST_LEN = 256

GRID_W = 64
HEAD_DIM = 64
MIX_WIDTH = D_MODEL
GROUP_WIDTH = MIX_WIDTH // 4
ATTN_HEADS = GROUP_WIDTH // HEAD_DIM
ATTN_KV_HEADS = ATTN_HEADS // 2
ATTN_GROUP = ATTN_HEADS // ATTN_KV_HEADS
NA_HEADS = GROUP_WIDTH // HEAD_DIM
NA_KH = 8
NA_KW = 16
FOURIER_HEADS = 4
FOURIER_HEAD_W = GROUP_WIDTH // FOURIER_HEADS
POOL_WINDOWS = (2, 4, 8, 16)
POOL_GROUP_W = GROUP_WIDTH // len(POOL_WINDOWS)
Q_BLOCK = 128
ROPE_THETA = 10000.0
ATTN_SCALE = HEAD_DIM ** -0.5
PEER_HEADS = 8
PEER_N_KEYS = 128
PEER_N_EXPERTS = PEER_N_KEYS * PEER_N_KEYS
PEER_KEY_DIM = 256
PEER_TOPK = 16
PEER_CHUNK = 128
ALPHA = (2 * DEPTH) ** 0.25
BETA = (8 * DEPTH) ** -0.25
LN_EPS = 1e-6
RMS_EPS = 1e-6
ADA_CHUNKS = 6
A_Q_W = ATTN_HEADS * HEAD_DIM
A_KV_W = ATTN_KV_HEADS * HEAD_DIM
IN_SPLITS = (A_Q_W, A_KV_W, A_KV_W, GROUP_WIDTH, GROUP_WIDTH, GROUP_WIDTH, GROUP_WIDTH, GROUP_WIDTH)
P_IN = A_Q_W + 2 * A_KV_W + 5 * GROUP_WIDTH

kernel_name = "hybrid_diffusion_peer_step"


def layer_norm(x):
    xf = x.astype(jnp.float32)
    mu = xf.mean(-1, keepdims=True)
    var = jnp.square(xf - mu).mean(-1, keepdims=True)
    return (xf - mu) * lax.rsqrt(var + LN_EPS)


def modulate(x, shift, scale):
    return (layer_norm(x) * (1.0 + scale) + shift).astype(x.dtype)


def post_norm(z, g, b):
    return (layer_norm(z) * g + b).astype(z.dtype)


def head_rms_norm(x, g):
    xf = x.astype(jnp.float32)
    y = xf * lax.rsqrt(jnp.mean(jnp.square(xf), -1, keepdims=True) + RMS_EPS) * g
    return y.astype(x.dtype)


def axial_rope(x):
    L = x.shape[1]
    t = jnp.arange(L)
    row = (t // GRID_W).astype(jnp.float32)
    col = (t % GRID_W).astype(jnp.float32)
    nf = HEAD_DIM // 4
    inv = ROPE_THETA ** (-jnp.arange(nf, dtype=jnp.float32) / nf)

    def rot(xh, pos):
        ang = pos[:, None] * inv[None, :]
        cos = jnp.cos(ang)[:, None, :]
        sin = jnp.sin(ang)[:, None, :]
        x1, x2 = jnp.split(xh, 2, axis=-1)
        return jnp.concatenate([x1 * cos - x2 * sin, x1 * sin + x2 * cos], -1)

    xf = x.astype(jnp.float32)
    half = HEAD_DIM // 2
    return jnp.concatenate([rot(xf[..., :half], row), rot(xf[..., half:], col)], -1).astype(x.dtype)


def blocked_attention(q, k, v):
    B, Lq, Hkv, G, hd = q.shape
    nb = Lq // Q_BLOCK
    q_blocks = q.reshape(B, nb, Q_BLOCK, Hkv, G, hd).swapaxes(0, 1)

    def one_block(qb):
        s = jnp.einsum('bqhgd,bkhd->bhgqk', qb, k).astype(jnp.float32) * ATTN_SCALE
        p = jax.nn.softmax(s, axis=-1).astype(v.dtype)
        return jnp.einsum('bhgqk,bkhd->bqhgd', p, v)

    out = lax.map(one_block, q_blocks)
    return out.swapaxes(0, 1).reshape(B, Lq, Hkv * G * hd)


def neighbourhood_attention(q, k, v, k_ctx, v_ctx, bias_table):
    B, L, H, hd = q.shape
    rows = L // GRID_W
    kh = min(NA_KH, rows)
    n_win = kh * NA_KW
    row_ids = jnp.arange(rows)
    row_start = jnp.clip(row_ids - kh // 2, 0, rows - kh)
    col_ids = jnp.arange(GRID_W)
    col_keys = jnp.clip(col_ids - NA_KW // 2, 0, GRID_W - NA_KW)[:, None] + jnp.arange(NA_KW)[None, :]
    col_rel = col_keys - col_ids[:, None] + (NA_KW - 1)
    col_bias = bias_table[:, :, col_rel]
    kg = k.reshape(B, rows, GRID_W, H, hd)
    vg = v.reshape(B, rows, GRID_W, H, hd)
    q_rows = q.reshape(B, rows, GRID_W, H, hd).swapaxes(0, 1)

    def one_row(args):
        q_r, r, rs = args
        k_r = lax.dynamic_slice_in_dim(kg, rs, kh, axis=1)[:, :, col_keys]
        v_r = lax.dynamic_slice_in_dim(vg, rs, kh, axis=1)[:, :, col_keys]
        row_rel = rs + jnp.arange(kh) - r + (NA_KH - 1)
        bias = jnp.take(col_bias, row_rel, axis=1).transpose(0, 2, 1, 3)
        s_win = jnp.einsum('bqhd,bkqjhd->bhqkj', q_r, k_r).astype(jnp.float32) * ATTN_SCALE
        s_win = s_win + bias.astype(jnp.float32)[None]
        s_ctx = jnp.einsum('bqhd,bchd->bhqc', q_r, k_ctx).astype(jnp.float32) * ATTN_SCALE
        s = jnp.concatenate([s_win.reshape(B, H, GRID_W, n_win), s_ctx], axis=-1)
        p = jax.nn.softmax(s, axis=-1).astype(v.dtype)
        p_win = p[..., :n_win].reshape(B, H, GRID_W, kh, NA_KW)
        return (jnp.einsum('bhqkj,bkqjhd->bqhd', p_win, v_r)
                + jnp.einsum('bhqc,bchd->bqhd', p[..., n_win:], v_ctx))

    out = lax.map(one_row, (q_rows, row_ids, row_start))
    return out.swapaxes(0, 1).reshape(B, L, H * hd)


def fourier_mix(x, w):
    B, L, _ = x.shape
    xh = x.astype(jnp.float32).reshape(B, L, FOURIER_HEADS, FOURIER_HEAD_W)
    y = jnp.fft.fftn(xh, axes=(1, 3), norm="ortho").real
    return y.reshape(B, L, GROUP_WIDTH).astype(x.dtype) @ w


def multiscale_pool(x, w_pool, pool_scale):
    B, L, C = x.shape
    xf = x.astype(jnp.float32)
    cs = jnp.concatenate([jnp.zeros((B, 1, C), jnp.float32), jnp.cumsum(xf, axis=1)], axis=1)
    t = jnp.arange(L)
    outs = []
    for g, w in enumerate(POOL_WINDOWS):
        lo = jnp.clip(t - w // 2, 0, L)
        hi = jnp.clip(t - w // 2 + w, 0, L)
        sl = slice(g * POOL_GROUP_W, (g + 1) * POOL_GROUP_W)
        csg = cs[..., sl]
        mean = (csg[:, hi] - csg[:, lo]) / (hi - lo).astype(jnp.float32)[None, :, None]
        outs.append(mean - xf[..., sl])
    pooled = jnp.stack(outs, axis=2).astype(x.dtype)
    y = jnp.einsum('blgc,gcd->blgd', pooled, w_pool).reshape(B, L, C)
    return y * pool_scale


def peer(h, wq, k1, k2, u, v):
    B, L, D = h.shape
    T = B * L
    x = h.reshape(T, D)
    q = (x @ wq).reshape(T, PEER_HEADS, 2, PEER_KEY_DIM // 2)
    s1 = jnp.einsum('thd,nd->thn', q[:, :, 0], k1).astype(jnp.float32)
    s2 = jnp.einsum('thd,nd->thn', q[:, :, 1], k2).astype(jnp.float32)
    v1, i1 = lax.top_k(s1, PEER_TOPK)
    v2, i2 = lax.top_k(s2, PEER_TOPK)
    cand_s = (v1[..., :, None] + v2[..., None, :]).reshape(T, PEER_HEADS, PEER_TOPK * PEER_TOPK)
    cand_i = (i1[..., :, None] * PEER_N_KEYS + i2[..., None, :]).reshape(T, PEER_HEADS, PEER_TOPK * PEER_TOPK)
    top_s, pos = lax.top_k(cand_s, PEER_TOPK)
    idx = jnp.take_along_axis(cand_i, pos, axis=-1)
    g = jax.nn.softmax(top_s, axis=-1).astype(h.dtype)
    nck = T // PEER_CHUNK
    idx = idx.reshape(nck, PEER_CHUNK, PEER_HEADS * PEER_TOPK)
    g = g.reshape(nck, PEER_CHUNK, PEER_HEADS * PEER_TOPK)

    def one_chunk(args):
        xc, ic, gc = args
        a = jnp.einsum('td,tkd->tk', xc, u[ic])
        return jnp.einsum('tk,tkd->td', gc * jax.nn.gelu(a, approximate=False), v[ic])

    y = lax.map(one_chunk, (x.reshape(nck, PEER_CHUNK, D), idx, g))
    return y.reshape(B, L, D)


def trunk_layer(x, ada, lw, ctx_cache):
    B, L, _ = x.shape
    sh1, sc1, g1, sh2, sc2, g2 = jnp.split(ada, ADA_CHUNKS, axis=-1)
    h = modulate(x, sh1, sc1)
    proj = h @ lw["w_in"]
    parts = []
    off = 0
    for size in IN_SPLITS:
        parts.append(proj[..., off:off + size])
        off += size
    aq, ak, av, fx, nq, nk, nv, px = parts
    aq = head_rms_norm(aq.reshape(B, L, ATTN_HEADS, HEAD_DIM), lw["q_norm_g"])
    ak = head_rms_norm(ak.reshape(B, L, ATTN_KV_HEADS, HEAD_DIM), lw["k_norm_g"])
    av = av.reshape(B, L, ATTN_KV_HEADS, HEAD_DIM)
    nq = nq.reshape(B, L, NA_HEADS, HEAD_DIM)
    nk = nk.reshape(B, L, NA_HEADS, HEAD_DIM)
    nv = nv.reshape(B, L, NA_HEADS, HEAD_DIM)
    if ctx_cache is None:
        a_out = blocked_attention(aq.reshape(B, L, ATTN_KV_HEADS, ATTN_GROUP, HEAD_DIM), ak, av)
        n_out = blocked_attention(nq.reshape(B, L, NA_HEADS, 1, HEAD_DIM), nk, nv)
        ctx_kv = (ak, av, nk, nv)
    else:
        ck_a, cv_a, ck_n, cv_n = ctx_cache
        aq = axial_rope(aq)
        ak = axial_rope(ak)
        a_out = blocked_attention(aq.reshape(B, L, ATTN_KV_HEADS, ATTN_GROUP, HEAD_DIM),
                                  jnp.concatenate([ak, ck_a], axis=1),
                                  jnp.concatenate([av, cv_a], axis=1))
        n_out = neighbourhood_attention(nq, nk, nv, ck_n, cv_n, lw["na_bias"])
        ctx_kv = None
    f_out = fourier_mix(fx, lw["w_fourier"])
    p_out = multiscale_pool(px, lw["w_pool"], lw["pool_scale"])
    mix = jnp.concatenate([a_out, f_out, n_out, p_out], axis=-1) @ lw["w_out"]
    x = post_norm(ALPHA * x + g1 * mix, lw["ln1_g"], lw["ln1_b"])
    h2 = modulate(x, sh2, sc2)
    ff = peer(h2, lw["peer_wq"], lw["peer_k1"], lw["peer_k2"], lw["peer_u"], lw["peer_v"])
    x = post_norm(ALPHA * x + g2 * ff, lw["ln2_g"], lw["ln2_b"])
    return x, ctx_kv


def setup_inputs(seed: int = 0) -> dict:
    key = jax.random.key(seed)
    ks = jax.random.split(key, 32)
    D = D_MODEL

    def nrm(k, shape, s):
        return jax.random.normal(k, shape, jnp.float32) * s

    return {
        "x_prompt": nrm(ks[0], (BATCH, SEQ, D), 1.0),
        "x_sample": nrm(ks[1], (DEC_BATCH, DEC_SEQ, D), 1.0),
        "cache_attn_k": nrm(ks[2], (DEC_BATCH, DEPTH, PAST_LEN, ATTN_KV_HEADS, HEAD_DIM), 1.0),
        "cache_attn_v": nrm(ks[3], (DEC_BATCH, DEPTH, PAST_LEN, ATTN_KV_HEADS, HEAD_DIM), 1.0),
        "cache_na_k": nrm(ks[4], (DEC_BATCH, DEPTH, PAST_LEN, NA_HEADS, HEAD_DIM), 1.0),
        "cache_na_v": nrm(ks[5], (DEC_BATCH, DEPTH, PAST_LEN, NA_HEADS, HEAD_DIM), 1.0),
        "c": nrm(ks[6], (DEC_BATCH, D), 1.0),
        "c_ctx": nrm(ks[7], (D,), 1.0),
        "w_ada": nrm(ks[8], (DEPTH, D, ADA_CHUNKS * D), 0.5 * D ** -0.5),
        "b_ada": nrm(ks[9], (DEPTH, ADA_CHUNKS * D), 0.01),
        "w_in": nrm(ks[10], (DEPTH, D, P_IN), D ** -0.5),
        "q_norm_g": 1.0 + nrm(ks[11], (DEPTH, HEAD_DIM), 0.02),
        "k_norm_g": 1.0 + nrm(ks[12], (DEPTH, HEAD_DIM), 0.02),
        "w_fourier": nrm(ks[13], (DEPTH, GROUP_WIDTH, GROUP_WIDTH), GROUP_WIDTH ** -0.5),
        "na_bias": nrm(ks[14], (DEPTH, NA_HEADS, 2 * NA_KH - 1, 2 * NA_KW - 1), 0.1),
        "w_pool": nrm(ks[15], (DEPTH, len(POOL_WINDOWS), POOL_GROUP_W, POOL_GROUP_W), POOL_GROUP_W ** -0.5),
        "pool_scale": 1.0 + nrm(ks[16], (DEPTH, GROUP_WIDTH), 0.02),
        "w_out": nrm(ks[17], (DEPTH, MIX_WIDTH, D), MIX_WIDTH ** -0.5 * BETA),
        "ln1_g": 1.0 + nrm(ks[18], (DEPTH, D), 0.02),
        "ln1_b": nrm(ks[19], (DEPTH, D), 0.02),
        "ln2_g": 1.0 + nrm(ks[20], (DEPTH, D), 0.02),
        "ln2_b": nrm(ks[21], (DEPTH, D), 0.02),
        "peer_wq": nrm(ks[22], (DEPTH, D, PEER_HEADS * PEER_KEY_DIM), D ** -0.5),
        "peer_k1": nrm(ks[23], (DEPTH, PEER_N_KEYS, PEER_KEY_DIM // 2), (PEER_KEY_DIM // 2) ** -0.5),
        "peer_k2": nrm(ks[24], (DEPTH, PEER_N_KEYS, PEER_KEY_DIM // 2), (PEER_KEY_DIM // 2) ** -0.5),
        "peer_u": nrm(ks[25], (DEPTH, PEER_N_EXPERTS, D), D ** -0.5),
        "peer_v": nrm(ks[26], (DEPTH, PEER_N_EXPERTS, D), BETA),
    }


def reference(x_prompt, x_sample, cache_attn_k, cache_attn_v, cache_na_k, cache_na_v, c, c_ctx,
              w_ada, b_ada, w_in, q_norm_g, k_norm_g, w_fourier, na_bias, w_pool, pool_scale,
              w_out, ln1_g, ln1_b, ln2_g, ln2_b, peer_wq, peer_k1, peer_k2, peer_u, peer_v):
    xp = x_prompt
    xs = x_sample
    ak_list, av_list, nk_list, nv_list = [], [], [], []
    for l in range(DEPTH):
        lw = {
            "w_in": w_in[l], "q_norm_g": q_norm_g[l], "k_norm_g": k_norm_g[l],
            "w_fourier": w_fourier[l], "na_bias": na_bias[l], "w_pool": w_pool[l],
            "pool_scale": pool_scale[l], "w_out": w_out[l],
            "ln1_g": ln1_g[l], "ln1_b": ln1_b[l], "ln2_g": ln2_g[l], "ln2_b": ln2_b[l],
            "peer_wq": peer_wq[l], "peer_k1": peer_k1[l], "peer_k2": peer_k2[l],
            "peer_u": peer_u[l], "peer_v": peer_v[l],
        }
        ada_ctx = (jax.nn.silu(c_ctx[None, :]) @ w_ada[l] + b_ada[l])[:, None, :]
        ada_lat = (jax.nn.silu(c) @ w_ada[l] + b_ada[l])[:, None, :]
        xp, (ak, av, nk, nv) = trunk_layer(xp, ada_ctx, lw, None)
        ak_list.append(ak)
        av_list.append(av)
        nk_list.append(nk)
        nv_list.append(nv)
        xs, _ = trunk_layer(xs, ada_lat, lw,
                            (cache_attn_k[:, l], cache_attn_v[:, l], cache_na_k[:, l], cache_na_v[:, l]))
    new_attn_k = jnp.stack(ak_list, axis=1)
    new_attn_v = jnp.stack(av_list, axis=1)
    new_na_k = jnp.stack(nk_list, axis=1)
    new_na_v = jnp.stack(nv_list, axis=1)
    return (xp, xs, new_attn_k, new_attn_v, new_na_k, new_na_v)
```

```python
import functools
import math

import numpy as np
import jax
import jax.numpy as jnp
from jax import lax
from jax.experimental import pallas as pl
from jax.experimental.pallas import tpu as pltpu

D_MODEL = 1024
DEPTH = 2
GRID_W = 64
HEAD_DIM = 64
GROUP_WIDTH = 256
ATTN_HEADS = 4
ATTN_KV_HEADS = 2
NA_HEADS = 4
NA_KH = 8
NA_KW = 16
FOURIER_HEADS = 4
POOL_WINDOWS = (2, 4, 8, 16)
POOL_GROUP_W = GROUP_WIDTH // len(POOL_WINDOWS)
ROPE_THETA = 10000.0
ATTN_SCALE = HEAD_DIM ** -0.5
PEER_HEADS = 8
PEER_N_KEYS = 128
PEER_N_EXPERTS = PEER_N_KEYS * PEER_N_KEYS
PEER_KEY_DIM = 256
PEER_TOPK = 16
ALPHA = (2 * DEPTH) ** 0.25
LN_EPS = 1e-6
RMS_EPS = 1e-6
ADA_CHUNKS = 6
P_IN = 1792
COND_ROWS = 8

TOKEN_TILE = 256
ATTN_Q_TILE = 512
DFT_ROW_TILE = 512
PEER_TOKEN_TILE = 512
PEER_EXPERT_TILE = 1024
PEER_SUB_TILE = 256
PEER_LANE_CHUNK = 128
ADA_COL_TILE = 1024
NEG_BIG = -1e30
VMEM_LIMIT = 56 * 1024 * 1024

F32 = jnp.float32
BF16 = jnp.bfloat16


def _params(*sem):
    return pltpu.CompilerParams(dimension_semantics=sem, vmem_limit_bytes=VMEM_LIMIT)


def _dot(a, b):
    return jnp.dot(a.astype(BF16), b.astype(BF16), preferred_element_type=F32)


def _dot_nt(a, b):
    return lax.dot_general(a.astype(BF16), b.astype(BF16), (((1,), (1,)), ((), ())),
                           preferred_element_type=F32)


def _layer_norm(x):
    mu = jnp.mean(x, axis=-1, keepdims=True)
    xc = x - mu
    var = jnp.mean(xc * xc, axis=-1, keepdims=True)
    return xc * lax.rsqrt(var + LN_EPS)


def _ada_kernel(cond_ref, w_ref, b_ref, o_ref):
    c = cond_ref[...]
    s = c / (1.0 + jnp.exp(-c))
    o_ref[...] = jnp.dot(s, w_ref[...], precision=lax.Precision.HIGHEST,
                         preferred_element_type=F32) + b_ref[...]


def _ada_call(cond, w_ada, b_ada):
    n = ADA_CHUNKS * D_MODEL
    return pl.pallas_call(
        _ada_kernel,
        out_shape=jax.ShapeDtypeStruct((DEPTH, COND_ROWS, n), F32),
        grid=(DEPTH, n // ADA_COL_TILE),
        in_specs=[
            pl.BlockSpec((COND_ROWS, D_MODEL), lambda l, j: (0, 0)),
            pl.BlockSpec((None, D_MODEL, ADA_COL_TILE), lambda l, j: (l, 0, j)),
            pl.BlockSpec((None, 1, ADA_COL_TILE), lambda l, j: (l, 0, j)),
        ],
        out_specs=pl.BlockSpec((None, COND_ROWS, ADA_COL_TILE), lambda l, j: (l, 0, j)),
        compiler_params=_params("parallel", "arbitrary"),
        name="ada_proj",
    )(cond, w_ada, b_ada.reshape(DEPTH, 1, n))


def _head_mean_sq(x, g):
    sq = x * x
    hi = sq.astype(BF16)
    lo = (sq - hi.astype(F32)).astype(BF16)
    return (jnp.dot(hi, g, preferred_element_type=F32)
            + jnp.dot(lo, g, preferred_element_type=F32))


def _rotary(x, cos, sin_signed):
    w = x.shape[-1]
    fwd = pltpu.roll(x, w - 16, axis=1)
    bwd = pltpu.roll(x, 16, axis=1)
    lane = lax.broadcasted_iota(jnp.int32, x.shape, 1)
    partner = jnp.where((lane % 32) < 16, fwd, bwd)
    return x * cos + partner * sin_signed


def _inproj_kernel(x_ref, ada_ref, w_ref, gq_ref, gk_ref, hm_ref, cos_ref, sin_ref,
                   aq_ref, ak_ref, av_ref, fx_ref, nq_ref, nk_ref, nv_ref, px_ref, *, rope):
    x = x_ref[...]
    sh = ada_ref[:, 0:D_MODEL]
    sc = ada_ref[:, D_MODEL:2 * D_MODEL]
    h = _layer_norm(x) * (1.0 + sc) + sh
    proj = _dot(h, w_ref[...])
    aq = proj[:, 0:256]
    ak = proj[:, 256:384]
    hm = hm_ref[...]
    aq = aq * lax.rsqrt(_head_mean_sq(aq, hm) + RMS_EPS) * gq_ref[...]
    ak = ak * lax.rsqrt(_head_mean_sq(ak, hm[0:128, 0:128]) + RMS_EPS) * gk_ref[...]
    if rope:
        cos = cos_ref[...]
        sin = sin_ref[...]
        aq = _rotary(aq, cos, sin)
        ak = _rotary(ak, cos[:, 0:128], sin[:, 0:128])
    for hd in range(ATTN_HEADS):
        aq_ref[hd] = aq[:, hd * 64:(hd + 1) * 64]
    for hd in range(ATTN_KV_HEADS):
        ak_ref[hd] = ak[:, hd * 64:(hd + 1) * 64]
        av_ref[hd] = proj[:, 384 + hd * 64:384 + (hd + 1) * 64]
    fx_ref[...] = proj[:, 512:768]
    for hd in range(NA_HEADS):
        nq_ref[hd] = proj[:, 768 + hd * 64:768 + (hd + 1) * 64]
        nk_ref[hd] = proj[:, 1024 + hd * 64:1024 + (hd + 1) * 64]
        nv_ref[hd] = proj[:, 1280 + hd * 64:1280 + (hd + 1) * 64]
    px_ref[...] = proj[:, 1536:1792]


def _inproj_call(x, ada, cond_base, cond_span, seq_len, w_in, gq, gk, head_mean, cos_t, sin_t,
                 rope):
    t = x.shape[0]
    tm = TOKEN_TILE
    tiles_per_seq = seq_len // tm
    tiles_per_cond = cond_span // tm
    hm = lambda n: jax.ShapeDtypeStruct((n, t, HEAD_DIM), F32)
    hspec = lambda n: pl.BlockSpec((n, tm, HEAD_DIM), lambda i: (0, i, 0))
    const = lambda shape: pl.BlockSpec(shape, lambda i: tuple(0 for _ in shape))
    return pl.pallas_call(
        functools.partial(_inproj_kernel, rope=rope),
        out_shape=(hm(4), hm(2), hm(2), jax.ShapeDtypeStruct((t, 256), F32),
                   hm(4), hm(4), hm(4), jax.ShapeDtypeStruct((t, 256), F32)),
        grid=(t // tm,),
        in_specs=[
            pl.BlockSpec((tm, D_MODEL), lambda i: (i, 0)),
            pl.BlockSpec((None, 1, ADA_CHUNKS * D_MODEL),
                         lambda i: (cond_base + i // tiles_per_cond, 0, 0)),
            const((D_MODEL, P_IN)),
            const((1, 256)), const((1, 128)), const((256, 256)),
            pl.BlockSpec((tm, 256), lambda i: (i % tiles_per_seq, 0)),
            pl.BlockSpec((tm, 256), lambda i: (i % tiles_per_seq, 0)),
        ],
        out_specs=(hspec(4), hspec(2), hspec(2), pl.BlockSpec((tm, 256), lambda i: (i, 0)),
                   hspec(4), hspec(4), hspec(4), pl.BlockSpec((tm, 256), lambda i: (i, 0))),
        compiler_params=_params("parallel"),
        name="inproj_rope" if rope else "inproj",
    )(x, ada, w_in, gq, gk, head_mean, cos_t, sin_t)


def _softmax_attend(q, k, v):
    s = _dot_nt(q, k) * ATTN_SCALE
    m = jnp.max(s, axis=-1, keepdims=True)
    p = jnp.exp(s - m)
    return _dot(p, v) / jnp.sum(p, axis=-1, keepdims=True)


def _ctx_attn_kernel(aq_ref, ak_ref, av_ref, nq_ref, nk_ref, nv_ref, a_ref, n_ref):
    outs = [_softmax_attend(aq_ref[hd], ak_ref[hd // 2], av_ref[hd // 2])
            for hd in range(ATTN_HEADS)]
    a_ref[...] = jnp.concatenate(outs, axis=-1)
    outs = [_softmax_attend(nq_ref[hd], nk_ref[hd], nv_ref[hd]) for hd in range(NA_HEADS)]
    n_ref[...] = jnp.concatenate(outs, axis=-1)


def _ctx_attn_call(aq, ak, av, nq, nk, nv, batch, seq_len):
    t = batch * seq_len
    hspec = lambda n: pl.BlockSpec((n, seq_len, HEAD_DIM), lambda b: (0, b, 0))
    ospec = pl.BlockSpec((seq_len, 256), lambda b: (b, 0))
    return pl.pallas_call(
        _ctx_attn_kernel,
        out_shape=(jax.ShapeDtypeStruct((t, 256), F32), jax.ShapeDtypeStruct((t, 256), F32)),
        grid=(batch,),
        in_specs=[hspec(4), hspec(2), hspec(2), hspec(4), hspec(4), hspec(4)],
        out_specs=(ospec, ospec),
        compiler_params=_params("parallel"),
        name="ctx_attention",
    )(aq, ak, av, nq, nk, nv)


def _lat_attn_kernel(q_ref, k_ref, v_ref, kc_ref, vc_ref, o_ref):
    k = k_ref[...]
    v = v_ref[...]
    kc = kc_ref[...]
    vc = vc_ref[...]
    outs = []
    for g in range(2):
        q = q_ref[g]
        s_l = _dot_nt(q, k) * ATTN_SCALE
        s_c = _dot_nt(q, kc) * ATTN_SCALE
        m = jnp.maximum(jnp.max(s_l, axis=-1, keepdims=True), jnp.max(s_c, axis=-1, keepdims=True))
        p_l = jnp.exp(s_l - m)
        p_c = jnp.exp(s_c - m)
        den = jnp.sum(p_l, axis=-1, keepdims=True) + jnp.sum(p_c, axis=-1, keepdims=True)
        outs.append((_dot(p_l, v) + _dot(p_c, vc)) / den)
    o_ref[...] = jnp.concatenate(outs, axis=-1)


def _lat_attn_call(aq, ak, av, kc, vc, batch, seq_len):
    t = batch * seq_len
    tq = ATTN_Q_TILE
    nqb = seq_len // tq
    return pl.pallas_call(
        _lat_attn_kernel,
        out_shape=jax.ShapeDtypeStruct((t, 256), F32),
        grid=(batch, ATTN_KV_HEADS, nqb),
        in_specs=[
            pl.BlockSpec((2, tq, HEAD_DIM), lambda b, g, i: (g, b * nqb + i, 0)),
            pl.BlockSpec((None, seq_len, HEAD_DIM), lambda b, g, i: (g, b, 0)),
            pl.BlockSpec((None, seq_len, HEAD_DIM), lambda b, g, i: (g, b, 0)),
            pl.BlockSpec((None, None, kc.shape[2], HEAD_DIM), lambda b, g, i: (b, g, 0, 0)),
            pl.BlockSpec((None, None, kc.shape[2], HEAD_DIM), lambda b, g, i: (b, g, 0, 0)),
        ],
        out_specs=pl.BlockSpec((tq, 128), lambda b, g, i: (b * nqb + i, g)),
        compiler_params=_params("parallel", "parallel", "arbitrary"),
        name="lat_attention",
    )(aq, ak, av, kc, vc)


def _na_kernel(q_ref, k_ref, v_ref, kc_ref, vc_ref, bias_ref, o_ref, *, rows):
    kh = min(NA_KH, rows)

    def one_row(r, carry):
        rs = jnp.clip(r - kh // 2, 0, rows - kh)
        q_off = pl.multiple_of(r * GRID_W, GRID_W)
        w_off = pl.multiple_of(rs * GRID_W, GRID_W)
        outs = []
        for hd in range(2):
            q = q_ref[hd, pl.ds(q_off, GRID_W), :]
            kw = k_ref[hd, pl.ds(w_off, kh * GRID_W), :]
            vw = v_ref[hd, pl.ds(w_off, kh * GRID_W), :]
            s_w = _dot_nt(q, kw) * ATTN_SCALE + bias_ref[hd, r - rs]
            s_c = _dot_nt(q, kc_ref[hd]) * ATTN_SCALE
            m = jnp.maximum(jnp.max(s_w, axis=-1, keepdims=True),
                            jnp.max(s_c, axis=-1, keepdims=True))
            p_w = jnp.exp(s_w - m)
            p_c = jnp.exp(s_c - m)
            den = jnp.sum(p_w, axis=-1, keepdims=True) + jnp.sum(p_c, axis=-1, keepdims=True)
            outs.append((_dot(p_w, vw) + _dot(p_c, vc_ref[hd])) / den)
        o_ref[pl.ds(q_off, GRID_W), :] = jnp.concatenate(outs, axis=-1)
        return carry

    lax.fori_loop(0, rows, one_row, 0)


def _na_call(nq, nk, nv, kc, vc, bias, batch, seq_len):
    t = batch * seq_len
    rows = seq_len // GRID_W
    lspec = pl.BlockSpec((2, seq_len, HEAD_DIM), lambda b, hp: (hp, b, 0))
    cspec = pl.BlockSpec((None, 2, kc.shape[2], HEAD_DIM), lambda b, hp: (b, hp, 0, 0))
    return pl.pallas_call(
        functools.partial(_na_kernel, rows=rows),
        out_shape=jax.ShapeDtypeStruct((t, 256), F32),
        grid=(batch, NA_HEADS // 2),
        in_specs=[lspec, lspec, lspec, cspec, cspec,
                  pl.BlockSpec((2,) + bias.shape[1:], lambda b, hp: (hp, 0, 0, 0))],
        out_specs=pl.BlockSpec((seq_len, 128), lambda b, hp: (b, hp)),
        compiler_params=_params("parallel", "parallel"),
        name="neighbourhood_attention",
    )(nq, nk, nv, kc, vc, bias)


def _na_bias_patterns(bias_table, rows):
    kh = min(NA_KH, rows)
    c = np.arange(GRID_W)
    cs = np.clip(c - NA_KW // 2, 0, GRID_W - NA_KW)
    kc = np.arange(GRID_W)
    inside = (kc[None, :] >= cs[:, None]) & (kc[None, :] < cs[:, None] + NA_KW)
    col_rel = np.clip(kc[None, :] - c[:, None] + (NA_KW - 1), 0, 2 * NA_KW - 2)
    pi = np.arange(kh)
    k = np.arange(kh)
    row_rel = np.clip(k[None, :] + (NA_KH - 1) - pi[:, None], 0, 2 * NA_KH - 2)
    b = bias_table[:, row_rel][:, :, :, col_rel]
    b = jnp.where(inside[None, None, None], b, NEG_BIG)
    b = b.transpose(0, 1, 3, 2, 4)
    return b.reshape(bias_table.shape[0], kh, GRID_W, kh * GRID_W)


def _fourier_kernel(x_ref, cl_ref, sl_ref, cc_ref, sc_ref, w_ref, o_ref, *, norm):
    x = x_ref[...]
    xc = _dot(x, cc_ref[...])
    xs = _dot(x, sc_ref[...])
    y = (_dot(cl_ref[...], xc) - _dot(sl_ref[...], xs)) * norm
    o_ref[...] = _dot(y, w_ref[...])


def _fourier_call(fx, w_fourier, batch, seq_len):
    t = batch * seq_len
    tm = min(DFT_ROW_TILE, seq_len)
    nmb = seq_len // tm
    j = np.arange(seq_len, dtype=np.float64)
    ang_l = 2.0 * np.pi * ((j[:, None] * j[None, :]) % seq_len) / seq_len
    hw = GROUP_WIDTH // FOURIER_HEADS
    c = np.arange(hw, dtype=np.float64)
    ang_c = 2.0 * np.pi * ((c[:, None] * c[None, :]) % hw) / hw
    eye = np.eye(FOURIER_HEADS)
    cl = jnp.asarray(np.cos(ang_l), F32)
    sl = jnp.asarray(np.sin(ang_l), F32)
    cc = jnp.asarray(np.kron(eye, np.cos(ang_c)), F32)
    sc = jnp.asarray(np.kron(eye, np.sin(ang_c)), F32)
    const = lambda shape: pl.BlockSpec(shape, lambda i, b: (0, 0))
    return pl.pallas_call(
        functools.partial(_fourier_kernel, norm=float((seq_len * hw) ** -0.5)),
        out_shape=jax.ShapeDtypeStruct((t, 256), F32),
        grid=(nmb, batch),
        in_specs=[
            pl.BlockSpec((seq_len, 256), lambda i, b: (b, 0)),
            pl.BlockSpec((tm, seq_len), lambda i, b: (i, 0)),
            pl.BlockSpec((tm, seq_len), lambda i, b: (i, 0)),
            const((256, 256)), const((256, 256)), const((256, 256)),
        ],
        out_specs=pl.BlockSpec((tm, 256), lambda i, b: (b * nmb + i, 0)),
        compiler_params=_params("parallel", "arbitrary"),
        name="fourier_mix",
    )(fx, cl, sl, cc, sc, w_fourier)


POOL_PAD = 8


def _pool_kernel(x_ref, w_ref, s_ref, o_ref, pad_ref, *, seq_len):
    x = x_ref[...]
    zeros = jnp.zeros((POOL_PAD, GROUP_WIDTH), F32)
    pad_ref[0:POOL_PAD, :] = zeros
    pad_ref[POOL_PAD + seq_len:2 * POOL_PAD + seq_len, :] = zeros
    pad_ref[POOL_PAD:POOL_PAD + seq_len, :] = x

    def shifted(off):
        return pad_ref[POOL_PAD + off:POOL_PAD + off + seq_len, :]

    t = lax.broadcasted_iota(jnp.int32, (seq_len, GROUP_WIDTH), 0)
    lane_group = lax.broadcasted_iota(jnp.int32, (seq_len, GROUP_WIDTH), 1) // POOL_GROUP_W
    acc = x
    done = 0
    pooled = jnp.zeros_like(x)
    for g, w in enumerate(POOL_WINDOWS):
        half = w // 2
        for off in list(range(-half, -done)) + list(range(max(done, 1), half)):
            acc = acc + shifted(off)
        done = half
        cnt = jnp.minimum(t + half, seq_len) - jnp.maximum(t - half, 0)
        pooled = jnp.where(lane_group == g, acc / cnt.astype(F32), pooled)
    o_ref[...] = _dot(pooled - x, w_ref[...]) * s_ref[...]


def _pool_call(px, w_pool_bd, pool_scale, batch, seq_len):
    t = batch * seq_len
    return pl.pallas_call(
        functools.partial(_pool_kernel, seq_len=seq_len),
        out_shape=jax.ShapeDtypeStruct((t, 256), F32),
        grid=(batch,),
        in_specs=[
            pl.BlockSpec((seq_len, 256), lambda b: (b, 0)),
            pl.BlockSpec((256, 256), lambda b: (0, 0)),
            pl.BlockSpec((1, 256), lambda b: (0, 0)),
        ],
        out_specs=pl.BlockSpec((seq_len, 256), lambda b: (b, 0)),
        scratch_shapes=[pltpu.VMEM((seq_len + 2 * POOL_PAD, GROUP_WIDTH), F32)],
        compiler_params=_params("parallel"),
        name="multiscale_pool",
    )(px, w_pool_bd, pool_scale)


def _outproj_kernel(a_ref, f_ref, n_ref, p_ref, x_ref, ada_ref, w_ref, g_ref, b_ref, o_ref):
    mix = (_dot(a_ref[...], w_ref[0:256, :]) + _dot(f_ref[...], w_ref[256:512, :])
           + _dot(n_ref[...], w_ref[512:768, :]) + _dot(p_ref[...], w_ref[768:1024, :]))
    gate = ada_ref[:, 2 * D_MODEL:3 * D_MODEL]
    z = ALPHA * x_ref[...] + gate * mix
    o_ref[...] = _layer_norm(z) * g_ref[...] + b_ref[...]


def _outproj_call(a_out, f_out, n_out, p_out, x, ada, cond_base, cond_span, w_out, ln_g, ln_b):
    t = x.shape[0]
    tm = TOKEN_TILE
    tiles_per_cond = cond_span // tm
    part = pl.BlockSpec((tm, 256), lambda i: (i, 0))
    row = pl.BlockSpec((1, D_MODEL), lambda i: (0, 0))
    return pl.pallas_call(
        _outproj_kernel,
        out_shape=jax.ShapeDtypeStruct((t, D_MODEL), F32),
        grid=(t // tm,),
        in_specs=[part, part, part, part,
                  pl.BlockSpec((tm, D_MODEL), lambda i: (i, 0)),
                  pl.BlockSpec((None, 1, ADA_CHUNKS * D_MODEL),
                               lambda i: (cond_base + i // tiles_per_cond, 0, 0)),
                  pl.BlockSpec((D_MODEL, D_MODEL), lambda i: (0, 0)), row, row],
        out_specs=pl.BlockSpec((tm, D_MODEL), lambda i: (i, 0)),
        compiler_params=_params("parallel"),
        name="outproj_postnorm",
    )(a_out, f_out, n_out, p_out, x, ada, w_out, ln_g, ln_b)


def _sorting_network(n):
    pairs = []
    p = 1
    while p < n:
        k = p
        while k >= 1:
            for j in range(k % p, n - k, 2 * k):
                for i in range(min(k, n - j - k)):
                    if (i + j) // (2 * p) == (i + j + k) // (2 * p):
                        pairs.append((i + j, i + j + k))
            k //= 2
        p *= 2
    return pairs


def _sort_descending(rows):
    rows = list(rows)
    for a, b in _sorting_network(len(rows)):
        rows[a], rows[b] = jnp.maximum(rows[a], rows[b]), jnp.minimum(rows[a], rows[b])
    return rows


def _sublane_max_all(x):
    x = jnp.maximum(x, pltpu.roll(x, 4, axis=0))
    x = jnp.maximum(x, pltpu.roll(x, 2, axis=0))
    return jnp.maximum(x, pltpu.roll(x, 1, axis=0))


def _extract_top(groups, count):
    groups = [list(g) for g in groups]
    out = []
    for it in range(count):
        head = groups[0][0]
        for g in groups[1:]:
            head = jnp.maximum(head, g[0])
        m = _sublane_max_all(head)
        out.append(m)
        remaining = count - it - 1
        if remaining == 0:
            break
        for g in groups:
            eq = g[0] == m
            keep = min(len(g), remaining)
            for i in range(keep):
                nxt = g[i + 1] if i + 1 < len(g) else jnp.full_like(m, -jnp.inf)
                g[i] = jnp.where(eq, nxt, g[i])
            del g[keep:]
    return out


def _peer_route(s1, s2):
    groups = PEER_N_KEYS // 8
    s1_rows = [s1[8 * i:8 * (i + 1)] for i in range(groups)]
    s2_rows = [s2[8 * i:8 * (i + 1)] for i in range(groups)]
    v1 = _extract_top([_sort_descending(s1_rows)], PEER_TOPK + 1)
    v2 = _extract_top([_sort_descending(s2_rows)], PEER_TOPK + 1)
    sub = lax.broadcasted_iota(jnp.int32, v1[0].shape, 0)
    lo = v1[0]
    hi = v1[8]
    for r in range(1, 8):
        lo = jnp.where(sub == r, v1[r], lo)
        hi = jnp.where(sub == r, v1[8 + r], hi)
    cand = [[lo + v2[b] for b in range(PEER_TOPK + 1)], [hi + v2[0]], [v1[PEER_TOPK] + v2[0]]]
    top = _extract_top(cand, PEER_TOPK + 1)
    den = jnp.ones_like(top[0])
    for tk in top[1:PEER_TOPK]:
        den = den + jnp.exp(tk - top[0])
    inv_den = 1.0 / den
    tau = 0.5 * (top[PEER_TOPK - 1] + top[PEER_TOPK])
    thr = [tau - r for r in s1_rows]
    coef = [jnp.exp(r - v1[0]) * inv_den for r in s1_rows]
    e2 = [jnp.exp(r - v2[0]) for r in s2_rows]
    return thr, coef, e2


def _peer_kernel(x_ref, ada_ref, wq_ref, k1_ref, k2_ref, u_ref, vt_ref, g_ref, b_ref, o_ref,
                 h_ref, ht_ref, yt_ref, act_ref, s2_ref, e2_ref, thr_ref, coef_ref):
    j = pl.program_id(1)
    tt = x_ref.shape[0]
    half = PEER_KEY_DIM // 2
    lanes = PEER_LANE_CHUNK

    @pl.when(j == 0)
    def _():
        sh = ada_ref[:, 3 * D_MODEL:4 * D_MODEL]
        sc = ada_ref[:, 4 * D_MODEL:5 * D_MODEL]
        h = _layer_norm(x_ref[...]) * (1.0 + sc) + sh
        h_ref[...] = h.astype(BF16)
        ht_ref[...] = h.T.astype(BF16)
        yt_ref[...] = jnp.zeros_like(yt_ref)

        def one_head(hd, carry):
            w_rows = wq_ref[pl.ds(pl.multiple_of(hd * PEER_KEY_DIM, PEER_KEY_DIM), PEER_KEY_DIM), :]
            q_t = lax.dot_general(w_rows, h_ref[...], (((1,), (1,)), ((), ())),
                                  preferred_element_type=F32)
            s1 = _dot(k1_ref[...], q_t[0:half])
            s2 = _dot(k2_ref[...], q_t[half:2 * half])
            s2_ref[hd] = s2
            for c in range(tt // lanes):
                cols = slice(c * lanes, (c + 1) * lanes)
                thr, coef, e2 = _peer_route(s1[:, cols], s2[:, cols])
                for i in range(PEER_N_KEYS // 8):
                    rows = pl.ds(8 * i, 8)
                    thr_ref[hd, rows, cols] = thr[i]
                    coef_ref[hd, rows, cols] = coef[i]
                    e2_ref[hd, rows, cols] = e2[i]
            return carry

        lax.fori_loop(0, PEER_HEADS, one_head, 0)

    keys_per_block = PEER_EXPERT_TILE // PEER_N_KEYS
    keys_per_sub = PEER_SUB_TILE // PEER_N_KEYS
    n_sub = PEER_EXPERT_TILE // PEER_SUB_TILE

    def up_proj(sub):
        return jnp.dot(u_ref[sub * PEER_SUB_TILE:(sub + 1) * PEER_SUB_TILE, :], ht_ref[...],
                       preferred_element_type=F32)

    acc = None
    a_next = up_proj(0)
    for sub in range(n_sub):
        sub_rows = slice(sub * PEER_SUB_TILE, (sub + 1) * PEER_SUB_TILE)
        a_sub = a_next
        if sub + 1 < n_sub:
            a_next = up_proj(sub + 1)
        for ii in range(keys_per_sub):
            i1 = j * keys_per_block + sub * keys_per_sub + ii
            rows = slice(ii * PEER_N_KEYS, (ii + 1) * PEER_N_KEYS)
            thr_rows = [thr_ref[hd, pl.ds(i1, 1), :] for hd in range(PEER_HEADS)]
            coef_rows = [coef_ref[hd, pl.ds(i1, 1), :] for hd in range(PEER_HEADS)]
            for c in range(tt // lanes):
                cols = slice(c * lanes, (c + 1) * lanes)
                w = None
                for hd in range(PEER_HEADS):
                    thr = thr_rows[hd][:, cols]
                    coef = coef_rows[hd][:, cols]
                    term = jnp.where(s2_ref[hd, :, cols] >= thr, e2_ref[hd, :, cols], 0.0) * coef
                    w = term if w is None else w + term
                a = a_sub[rows, cols]
                gelu = 0.5 * a * (1.0 + lax.erf(a * (2.0 ** -0.5)))
                act_ref[sub * PEER_SUB_TILE + ii * PEER_N_KEYS:
                        sub * PEER_SUB_TILE + (ii + 1) * PEER_N_KEYS, cols] = (w * gelu).astype(BF16)
        part = jnp.dot(vt_ref[:, sub_rows], act_ref[sub_rows, :], preferred_element_type=F32)
        acc = part if acc is None else acc + part
    yt_ref[...] += acc

    @pl.when(j == pl.num_programs(1) - 1)
    def _():
        gate = ada_ref[:, 5 * D_MODEL:6 * D_MODEL]
        z = ALPHA * x_ref[...] + gate * yt_ref[...].T
        o_ref[...] = _layer_norm(z) * g_ref[...] + b_ref[...]


def _peer_call(x, ada, cond_base, cond_span, wq_t, k1, k2, u, v_t, ln_g, ln_b):
    t = x.shape[0]
    tt = PEER_TOKEN_TILE
    tiles_per_cond = cond_span // tt
    eb = PEER_EXPERT_TILE
    row = pl.BlockSpec((1, D_MODEL), lambda i, j: (0, 0))
    per_head = pltpu.VMEM((PEER_HEADS, PEER_N_KEYS, tt), F32)
    return pl.pallas_call(
        _peer_kernel,
        out_shape=jax.ShapeDtypeStruct((t, D_MODEL), F32),
        grid=(t // tt, PEER_N_EXPERTS // eb),
        in_specs=[
            pl.BlockSpec((tt, D_MODEL), lambda i, j: (i, 0)),
            pl.BlockSpec((None, 1, ADA_CHUNKS * D_MODEL),
                         lambda i, j: (cond_base + i // tiles_per_cond, 0, 0)),
            pl.BlockSpec((PEER_HEADS * PEER_KEY_DIM, D_MODEL), lambda i, j: (0, 0)),
            pl.BlockSpec((PEER_N_KEYS, PEER_KEY_DIM // 2), lambda i, j: (0, 0)),
            pl.BlockSpec((PEER_N_KEYS, PEER_KEY_DIM // 2), lambda i, j: (0, 0)),
            pl.BlockSpec((eb, D_MODEL), lambda i, j: (j, 0)),
            pl.BlockSpec((D_MODEL, eb), lambda i, j: (0, j)),
            row, row,
        ],
        out_specs=pl.BlockSpec((tt, D_MODEL), lambda i, j: (i, 0)),
        scratch_shapes=[
            pltpu.VMEM((tt, D_MODEL), BF16),
            pltpu.VMEM((D_MODEL, tt), BF16),
            pltpu.VMEM((D_MODEL, tt), F32),
            pltpu.VMEM((eb, tt), BF16),
            per_head, per_head, per_head, per_head,
        ],
        compiler_params=_params("parallel", "arbitrary"),
        name="peer",
    )(x, ada, wq_t, k1, k2, u, v_t, ln_g, ln_b)


def _rope_tables(seq_len):
    t = np.arange(seq_len)
    row = (t // GRID_W).astype(np.float64)
    col = (t % GRID_W).astype(np.float64)
    nf = HEAD_DIM // 4
    inv = ROPE_THETA ** (-np.arange(nf, dtype=np.float64) / nf)
    d = np.arange(HEAD_DIM)
    pos = np.where(d[None, :] < HEAD_DIM // 2, row[:, None], col[:, None])
    ang = pos * inv[d % nf][None, :]
    sign = np.where((d % (HEAD_DIM // 2)) < nf, -1.0, 1.0)
    cos = np.tile(np.cos(ang), (1, ATTN_HEADS))
    sin = np.tile(np.sin(ang) * sign[None, :], (1, ATTN_HEADS))
    return jnp.asarray(cos, F32), jnp.asarray(sin, F32)


def _head_mean_matrix():
    m = np.kron(np.eye(ATTN_HEADS), np.full((HEAD_DIM, HEAD_DIM), 1.0 / HEAD_DIM))
    return jnp.asarray(m, BF16)


def _block_diag(w):
    g, a, b = w.shape
    out = jnp.zeros((g * a, g * b), w.dtype)
    for i in range(g):
        out = out.at[i * a:(i + 1) * a, i * b:(i + 1) * b].set(w[i])
    return out


def _trunk_layer(x, ada, cond_base, batch, seq_len, lw, ctx_cache):
    rope = ctx_cache is not None
    if rope:
        cos_t, sin_t = _rope_tables(seq_len)
    else:
        cos_t = jnp.ones((seq_len, 256), F32)
        sin_t = jnp.zeros((seq_len, 256), F32)
    cond_span = seq_len if rope else batch * seq_len
    aq, ak, av, fx, nq, nk, nv, px = _inproj_call(
        x, ada, cond_base, cond_span, seq_len, lw["w_in"], lw["gq"], lw["gk"], lw["head_mean"],
        cos_t, sin_t, rope)
    if ctx_cache is None:
        a_out, n_out = _ctx_attn_call(aq, ak, av, nq, nk, nv, batch, seq_len)
    else:
        ck_a, cv_a, ck_n, cv_n = ctx_cache
        a_out = _lat_attn_call(aq, ak, av, ck_a, cv_a, batch, seq_len)
        n_out = _na_call(nq, nk, nv, ck_n, cv_n, lw["na_bias"], batch, seq_len)
    f_out = _fourier_call(fx, lw["w_fourier"], batch, seq_len)
    p_out = _pool_call(px, lw["w_pool"], lw["pool_scale"], batch, seq_len)
    x = _outproj_call(a_out, f_out, n_out, p_out, x, ada, cond_base, cond_span,
                      lw["w_out"], lw["ln1_g"], lw["ln1_b"])
    x = _peer_call(x, ada, cond_base, cond_span, lw["wq_t"], lw["k1"], lw["k2"], lw["u"],
                   lw["v_t"], lw["ln2_g"], lw["ln2_b"])
    return x, (ak, av, nk, nv)


def kernel(x_prompt, x_sample, cache_attn_k, cache_attn_v, cache_na_k, cache_na_v, c, c_ctx,
           w_ada, b_ada, w_in, q_norm_g, k_norm_g, w_fourier, na_bias, w_pool, pool_scale,
           w_out, ln1_g, ln1_b, ln2_g, ln2_b, peer_wq, peer_k1, peer_k2, peer_u, peer_v):
    batch, seq, d = x_prompt.shape
    dec_batch, dec_seq, _ = x_sample.shape
    xp = x_prompt.reshape(batch * seq, d)
    xs = x_sample.reshape(dec_batch * dec_seq, d)

    cond = jnp.zeros((COND_ROWS, d), F32).at[0].set(c_ctx).at[1:1 + dec_batch].set(c)
    ada_all = _ada_call(cond, w_ada, b_ada)
    head_mean = _head_mean_matrix()
    rows = dec_seq // GRID_W

    def heads_first(cache):
        return cache.transpose(1, 0, 3, 2, 4)

    cak, cav = heads_first(cache_attn_k), heads_first(cache_attn_v)
    cnk, cnv = heads_first(cache_na_k), heads_first(cache_na_v)

    new_kv = [[], [], [], []]
    for l in range(DEPTH):
        lw = {
            "w_in": w_in[l].astype(BF16),
            "gq": jnp.tile(q_norm_g[l], ATTN_HEADS).reshape(1, -1),
            "gk": jnp.tile(k_norm_g[l], ATTN_KV_HEADS).reshape(1, -1),
            "head_mean": head_mean,
            "w_fourier": w_fourier[l].astype(BF16),
            "na_bias": _na_bias_patterns(na_bias[l], rows),
            "w_pool": _block_diag(w_pool[l]).astype(BF16),
            "pool_scale": pool_scale[l].reshape(1, -1),
            "w_out": w_out[l].astype(BF16),
            "ln1_g": ln1_g[l].reshape(1, -1), "ln1_b": ln1_b[l].reshape(1, -1),
            "ln2_g": ln2_g[l].reshape(1, -1), "ln2_b": ln2_b[l].reshape(1, -1),
            "wq_t": peer_wq[l].T.astype(BF16),
            "k1": peer_k1[l].astype(BF16), "k2": peer_k2[l].astype(BF16),
            "u": peer_u[l].astype(BF16),
            "v_t": peer_v[l].T.astype(BF16),
        }
        ada = ada_all[l].reshape(COND_ROWS, 1, ADA_CHUNKS * d)
        xp, kv = _trunk_layer(xp, ada, 0, batch, seq, lw, None)
        for dst, arr in zip(new_kv, kv):
            heads = arr.shape[0]
            dst.append(arr.reshape(heads, batch, seq, HEAD_DIM).transpose(1, 2, 0, 3))
        xs, _ = _trunk_layer(xs, ada, 1, dec_batch, dec_seq, lw,
                             (cak[l], cav[l], cnk[l], cnv[l]))
    outs = [jnp.stack(lst, axis=1) for lst in new_kv]
    return (xp.reshape(batch, seq, d), xs.reshape(dec_batch, dec_seq, d), *outs)
```

```python
import functools
import math

import numpy as np
import jax
import jax.numpy as jnp
from jax import lax
from jax.experimental import pallas as pl
from jax.experimental.pallas import tpu as pltpu

D_MODEL = 1024
DEPTH = 2
GRID_W = 64
HEAD_DIM = 64
GROUP_WIDTH = 256
ATTN_HEADS = 4
ATTN_KV_HEADS = 2
NA_HEADS = 4
NA_KH = 8
NA_KW = 16
FOURIER_HEADS = 4
POOL_WINDOWS = (2, 4, 8, 16)
POOL_GROUP_W = GROUP_WIDTH // len(POOL_WINDOWS)
ROPE_THETA = 10000.0
ATTN_SCALE = HEAD_DIM ** -0.5
PEER_HEADS = 8
PEER_N_KEYS = 128
PEER_N_EXPERTS = PEER_N_KEYS * PEER_N_KEYS
PEER_KEY_DIM = 256
PEER_TOPK = 16
ALPHA = (2 * DEPTH) ** 0.25
LN_EPS = 1e-6
RMS_EPS = 1e-6
ADA_CHUNKS = 6
P_IN = 1792
COND_ROWS = 8

TOKEN_TILE = 256
ATTN_Q_TILE = 512
DFT_ROW_TILE = 512
NA_ROWS_PER_ITER = 4
PEER_TOKEN_TILE = 512
PEER_EXPERT_TILE = 1024
PEER_SUB_TILE = 512
PEER_LANE_CHUNK = 128
ADA_COL_TILE = 1024
NEG_BIG = -1e30
VMEM_LIMIT = 56 * 1024 * 1024

F32 = jnp.float32
BF16 = jnp.bfloat16


def _params(*sem):
    return pltpu.CompilerParams(dimension_semantics=sem, vmem_limit_bytes=VMEM_LIMIT)


def _dot(a, b):
    return jnp.dot(a.astype(BF16), b.astype(BF16), preferred_element_type=F32)


def _dot_nt(a, b):
    return lax.dot_general(a.astype(BF16), b.astype(BF16), (((1,), (1,)), ((), ())),
                           preferred_element_type=F32)


def _layer_norm(x):
    mu = jnp.mean(x, axis=-1, keepdims=True)
    xc = x - mu
    var = jnp.mean(xc * xc, axis=-1, keepdims=True)
    return xc * lax.rsqrt(var + LN_EPS)


def _ada_kernel(cond_ref, w_ref, b_ref, o_ref):
    c = cond_ref[...]
    s = c / (1.0 + jnp.exp(-c))
    o_ref[...] = jnp.dot(s, w_ref[...], precision=lax.Precision.HIGHEST,
                         preferred_element_type=F32) + b_ref[...]


def _ada_call(cond, w_ada, b_ada):
    n = ADA_CHUNKS * D_MODEL
    return pl.pallas_call(
        _ada_kernel,
        out_shape=jax.ShapeDtypeStruct((DEPTH, COND_ROWS, n), F32),
        grid=(DEPTH, n // ADA_COL_TILE),
        in_specs=[
            pl.BlockSpec((COND_ROWS, D_MODEL), lambda l, j: (0, 0)),
            pl.BlockSpec((None, D_MODEL, ADA_COL_TILE), lambda l, j: (l, 0, j)),
            pl.BlockSpec((None, 1, ADA_COL_TILE), lambda l, j: (l, 0, j)),
        ],
        out_specs=pl.BlockSpec((None, COND_ROWS, ADA_COL_TILE), lambda l, j: (l, 0, j)),
        compiler_params=_params("parallel", "arbitrary"),
        name="ada_proj",
    )(cond, w_ada, b_ada.reshape(DEPTH, 1, n))


def _head_mean_sq(x, g):
    sq = x * x
    hi = sq.astype(BF16)
    lo = (sq - hi.astype(F32)).astype(BF16)
    return (jnp.dot(hi, g, preferred_element_type=F32)
            + jnp.dot(lo, g, preferred_element_type=F32))


def _rotary(x, cos, sin_signed):
    w = x.shape[-1]
    fwd = pltpu.roll(x, w - 16, axis=1)
    bwd = pltpu.roll(x, 16, axis=1)
    lane = lax.broadcasted_iota(jnp.int32, x.shape, 1)
    partner = jnp.where((lane % 32) < 16, fwd, bwd)
    return x * cos + partner * sin_signed


def _inproj_kernel(x_ref, ada_ref, w_ref, gq_ref, gk_ref, hm_ref, cos_ref, sin_ref,
                   aq_ref, ak_ref, av_ref, fx_ref, nq_ref, nk_ref, nv_ref, px_ref, *, rope):
    x = x_ref[...]
    sh = ada_ref[:, 0:D_MODEL]
    sc = ada_ref[:, D_MODEL:2 * D_MODEL]
    h = _layer_norm(x) * (1.0 + sc) + sh
    proj = _dot(h, w_ref[...])
    aq = proj[:, 0:256]
    ak = proj[:, 256:384]
    hm = hm_ref[...]
    aq = aq * lax.rsqrt(_head_mean_sq(aq, hm) + RMS_EPS) * gq_ref[...]
    ak = ak * lax.rsqrt(_head_mean_sq(ak, hm[0:128, 0:128]) + RMS_EPS) * gk_ref[...]
    if rope:
        cos = cos_ref[...]
        sin = sin_ref[...]
        aq = _rotary(aq, cos, sin)
        ak = _rotary(ak, cos[:, 0:128], sin[:, 0:128])
    for hd in range(ATTN_HEADS):
        aq_ref[hd] = aq[:, hd * 64:(hd + 1) * 64]
    for hd in range(ATTN_KV_HEADS):
        ak_ref[hd] = ak[:, hd * 64:(hd + 1) * 64]
        av_ref[hd] = proj[:, 384 + hd * 64:384 + (hd + 1) * 64]
    fx_ref[...] = proj[:, 512:768]
    for hd in range(NA_HEADS):
        nq_ref[hd] = proj[:, 768 + hd * 64:768 + (hd + 1) * 64]
        nk_ref[hd] = proj[:, 1024 + hd * 64:1024 + (hd + 1) * 64]
        nv_ref[hd] = proj[:, 1280 + hd * 64:1280 + (hd + 1) * 64]
    px_ref[...] = proj[:, 1536:1792]


def _inproj_call(x, ada, cond_base, cond_span, seq_len, w_in, gq, gk, head_mean, cos_t, sin_t,
                 rope):
    t = x.shape[0]
    tm = TOKEN_TILE
    tiles_per_seq = seq_len // tm
    tiles_per_cond = cond_span // tm
    hm = lambda n: jax.ShapeDtypeStruct((n, t, HEAD_DIM), F32)
    hspec = lambda n: pl.BlockSpec((n, tm, HEAD_DIM), lambda i: (0, i, 0))
    const = lambda shape: pl.BlockSpec(shape, lambda i: tuple(0 for _ in shape))
    return pl.pallas_call(
        functools.partial(_inproj_kernel, rope=rope),
        out_shape=(hm(4), hm(2), hm(2), jax.ShapeDtypeStruct((t, 256), F32),
                   hm(4), hm(4), hm(4), jax.ShapeDtypeStruct((t, 256), F32)),
        grid=(t // tm,),
        in_specs=[
            pl.BlockSpec((tm, D_MODEL), lambda i: (i, 0)),
            pl.BlockSpec((None, 1, ADA_CHUNKS * D_MODEL),
                         lambda i: (cond_base + i // tiles_per_cond, 0, 0)),
            const((D_MODEL, P_IN)),
            const((1, 256)), const((1, 128)), const((256, 256)),
            pl.BlockSpec((tm, 256), lambda i: (i % tiles_per_seq, 0)),
            pl.BlockSpec((tm, 256), lambda i: (i % tiles_per_seq, 0)),
        ],
        out_specs=(hspec(4), hspec(2), hspec(2), pl.BlockSpec((tm, 256), lambda i: (i, 0)),
                   hspec(4), hspec(4), hspec(4), pl.BlockSpec((tm, 256), lambda i: (i, 0))),
        compiler_params=_params("parallel"),
        name="inproj_rope" if rope else "inproj",
    )(x, ada, w_in, gq, gk, head_mean, cos_t, sin_t)


def _softmax_attend(q, k, v):
    s = _dot_nt(q, k) * ATTN_SCALE
    m = jnp.max(s, axis=-1, keepdims=True)
    p = jnp.exp(s - m)
    return _dot(p, v) / jnp.sum(p, axis=-1, keepdims=True)


def _ctx_attn_kernel(aq_ref, ak_ref, av_ref, nq_ref, nk_ref, nv_ref, a_ref, n_ref):
    outs = [_softmax_attend(aq_ref[hd], ak_ref[hd // 2], av_ref[hd // 2])
            for hd in range(ATTN_HEADS)]
    a_ref[...] = jnp.concatenate(outs, axis=-1)
    outs = [_softmax_attend(nq_ref[hd], nk_ref[hd], nv_ref[hd]) for hd in range(NA_HEADS)]
    n_ref[...] = jnp.concatenate(outs, axis=-1)


def _ctx_attn_call(aq, ak, av, nq, nk, nv, batch, seq_len):
    t = batch * seq_len
    hspec = lambda n: pl.BlockSpec((n, seq_len, HEAD_DIM), lambda b: (0, b, 0))
    ospec = pl.BlockSpec((seq_len, 256), lambda b: (b, 0))
    return pl.pallas_call(
        _ctx_attn_kernel,
        out_shape=(jax.ShapeDtypeStruct((t, 256), F32), jax.ShapeDtypeStruct((t, 256), F32)),
        grid=(batch,),
        in_specs=[hspec(4), hspec(2), hspec(2), hspec(4), hspec(4), hspec(4)],
        out_specs=(ospec, ospec),
        compiler_params=_params("parallel"),
        name="ctx_attention",
    )(aq, ak, av, nq, nk, nv)


def _lat_attn_kernel(q_ref, k_ref, v_ref, kc_ref, vc_ref, o_ref):
    k = k_ref[...]
    v = v_ref[...]
    kc = kc_ref[...]
    vc = vc_ref[...]
    outs = []
    for g in range(2):
        q = q_ref[g]
        s_l = _dot_nt(q, k) * ATTN_SCALE
        s_c = _dot_nt(q, kc) * ATTN_SCALE
        m = jnp.maximum(jnp.max(s_l, axis=-1, keepdims=True), jnp.max(s_c, axis=-1, keepdims=True))
        p_l = jnp.exp(s_l - m)
        p_c = jnp.exp(s_c - m)
        den = jnp.sum(p_l, axis=-1, keepdims=True) + jnp.sum(p_c, axis=-1, keepdims=True)
        outs.append((_dot(p_l, v) + _dot(p_c, vc)) / den)
    o_ref[...] = jnp.concatenate(outs, axis=-1)


def _lat_attn_call(aq, ak, av, kc, vc, batch, seq_len):
    t = batch * seq_len
    tq = ATTN_Q_TILE
    nqb = seq_len // tq
    return pl.pallas_call(
        _lat_attn_kernel,
        out_shape=jax.ShapeDtypeStruct((t, 256), F32),
        grid=(batch, ATTN_KV_HEADS, nqb),
        in_specs=[
            pl.BlockSpec((2, tq, HEAD_DIM), lambda b, g, i: (g, b * nqb + i, 0)),
            pl.BlockSpec((None, seq_len, HEAD_DIM), lambda b, g, i: (g, b, 0)),
            pl.BlockSpec((None, seq_len, HEAD_DIM), lambda b, g, i: (g, b, 0)),
            pl.BlockSpec((None, None, kc.shape[2], HEAD_DIM), lambda b, g, i: (b, g, 0, 0)),
            pl.BlockSpec((None, None, kc.shape[2], HEAD_DIM), lambda b, g, i: (b, g, 0, 0)),
        ],
        out_specs=pl.BlockSpec((tq, 128), lambda b, g, i: (b * nqb + i, g)),
        compiler_params=_params("parallel", "parallel", "arbitrary"),
        name="lat_attention",
    )(aq, ak, av, kc, vc)


def _na_kernel(q_ref, k_ref, v_ref, kc_ref, vc_ref, bias_ref, o_ref, *, rows):
    kh = min(NA_KH, rows)

    def one_row(r):
        rs = jnp.clip(r - kh // 2, 0, rows - kh)
        q_off = pl.multiple_of(r * GRID_W, GRID_W)
        w_off = pl.multiple_of(rs * GRID_W, GRID_W)
        outs = []
        for hd in range(2):
            q = q_ref[hd, pl.ds(q_off, GRID_W), :]
            kw = k_ref[hd, pl.ds(w_off, kh * GRID_W), :]
            vw = v_ref[hd, pl.ds(w_off, kh * GRID_W), :]
            s_w = _dot_nt(q, kw) * ATTN_SCALE + bias_ref[hd, r - rs]
            s_c = _dot_nt(q, kc_ref[hd]) * ATTN_SCALE
            m = jnp.maximum(jnp.max(s_w, axis=-1, keepdims=True),
                            jnp.max(s_c, axis=-1, keepdims=True))
            p_w = jnp.exp(s_w - m)
            p_c = jnp.exp(s_c - m)
            den = jnp.sum(p_w, axis=-1, keepdims=True) + jnp.sum(p_c, axis=-1, keepdims=True)
            outs.append((_dot(p_w, vw) + _dot(p_c, vc_ref[hd])) / den)
        o_ref[pl.ds(q_off, GRID_W), :] = jnp.concatenate(outs, axis=-1)

    def row_group(g, carry):
        for k in range(NA_ROWS_PER_ITER):
            one_row(g * NA_ROWS_PER_ITER + k)
        return carry

    lax.fori_loop(0, rows // NA_ROWS_PER_ITER, row_group, 0)


def _na_call(nq, nk, nv, kc, vc, bias, batch, seq_len):
    t = batch * seq_len
    rows = seq_len // GRID_W
    lspec = pl.BlockSpec((2, seq_len, HEAD_DIM), lambda b, hp: (hp, b, 0))
    cspec = pl.BlockSpec((None, 2, kc.shape[2], HEAD_DIM), lambda b, hp: (b, hp, 0, 0))
    return pl.pallas_call(
        functools.partial(_na_kernel, rows=rows),
        out_shape=jax.ShapeDtypeStruct((t, 256), F32),
        grid=(batch, NA_HEADS // 2),
        in_specs=[lspec, lspec, lspec, cspec, cspec,
                  pl.BlockSpec((2,) + bias.shape[1:], lambda b, hp: (hp, 0, 0, 0))],
        out_specs=pl.BlockSpec((seq_len, 128), lambda b, hp: (b, hp)),
        compiler_params=_params("parallel", "parallel"),
        name="neighbourhood_attention",
    )(nq, nk, nv, kc, vc, bias)


def _na_bias_patterns(bias_table, rows):
    kh = min(NA_KH, rows)
    c = np.arange(GRID_W)
    cs = np.clip(c - NA_KW // 2, 0, GRID_W - NA_KW)
    kc = np.arange(GRID_W)
    inside = (kc[None, :] >= cs[:, None]) & (kc[None, :] < cs[:, None] + NA_KW)
    col_rel = np.clip(kc[None, :] - c[:, None] + (NA_KW - 1), 0, 2 * NA_KW - 2)
    pi = np.arange(kh)
    k = np.arange(kh)
    row_rel = np.clip(k[None, :] + (NA_KH - 1) - pi[:, None], 0, 2 * NA_KH - 2)
    row_pick = (row_rel[:, :, None] == np.arange(2 * NA_KH - 1)).astype(np.float32)
    col_pick = (col_rel[:, :, None] == np.arange(2 * NA_KW - 1)).astype(np.float32)
    b = jnp.einsum("hrs,cqs->hrcq", bias_table, col_pick, precision=lax.Precision.HIGHEST)
    b = jnp.einsum("pkr,hrcq->hpckq", row_pick, b, precision=lax.Precision.HIGHEST)
    b = jnp.where(inside[None, None, :, None, :], b, NEG_BIG)
    return b.reshape(bias_table.shape[0], kh, GRID_W, kh * GRID_W)


def _fourier_kernel(x_ref, cl_ref, sl_ref, cc_ref, sc_ref, w_ref, o_ref, *, norm):
    x = x_ref[...]
    xc = _dot(x, cc_ref[...])
    xs = _dot(x, sc_ref[...])
    y = (_dot(cl_ref[...], xc) - _dot(sl_ref[...], xs)) * norm
    o_ref[...] = _dot(y, w_ref[...])


def _fourier_call(fx, w_fourier, batch, seq_len):
    t = batch * seq_len
    tm = min(DFT_ROW_TILE, seq_len)
    nmb = seq_len // tm
    j = np.arange(seq_len, dtype=np.float64)
    ang_l = 2.0 * np.pi * ((j[:, None] * j[None, :]) % seq_len) / seq_len
    hw = GROUP_WIDTH // FOURIER_HEADS
    c = np.arange(hw, dtype=np.float64)
    ang_c = 2.0 * np.pi * ((c[:, None] * c[None, :]) % hw) / hw
    eye = np.eye(FOURIER_HEADS)
    cl = jnp.asarray(np.cos(ang_l), F32)
    sl = jnp.asarray(np.sin(ang_l), F32)
    cc = jnp.asarray(np.kron(eye, np.cos(ang_c)), F32)
    sc = jnp.asarray(np.kron(eye, np.sin(ang_c)), F32)
    const = lambda shape: pl.BlockSpec(shape, lambda i, b: (0, 0))
    return pl.pallas_call(
        functools.partial(_fourier_kernel, norm=float((seq_len * hw) ** -0.5)),
        out_shape=jax.ShapeDtypeStruct((t, 256), F32),
        grid=(nmb, batch),
        in_specs=[
            pl.BlockSpec((seq_len, 256), lambda i, b: (b, 0)),
            pl.BlockSpec((tm, seq_len), lambda i, b: (i, 0)),
            pl.BlockSpec((tm, seq_len), lambda i, b: (i, 0)),
            const((256, 256)), const((256, 256)), const((256, 256)),
        ],
        out_specs=pl.BlockSpec((tm, 256), lambda i, b: (b * nmb + i, 0)),
        compiler_params=_params("parallel", "arbitrary"),
        name="fourier_mix",
    )(fx, cl, sl, cc, sc, w_fourier)


POOL_PAD = 8


def _pool_kernel(x_ref, w_ref, s_ref, o_ref, pad_ref, *, seq_len):
    x = x_ref[...]
    zeros = jnp.zeros((POOL_PAD, GROUP_WIDTH), F32)
    pad_ref[0:POOL_PAD, :] = zeros
    pad_ref[POOL_PAD + seq_len:2 * POOL_PAD + seq_len, :] = zeros
    pad_ref[POOL_PAD:POOL_PAD + seq_len, :] = x

    def shifted(off):
        return pad_ref[POOL_PAD + off:POOL_PAD + off + seq_len, :]

    t = lax.broadcasted_iota(jnp.int32, (seq_len, GROUP_WIDTH), 0)
    lane_group = lax.broadcasted_iota(jnp.int32, (seq_len, GROUP_WIDTH), 1) // POOL_GROUP_W
    acc = x
    done = 0
    pooled = jnp.zeros_like(x)
    for g, w in enumerate(POOL_WINDOWS):
        half = w // 2
        for off in list(range(-half, -done)) + list(range(max(done, 1), half)):
            acc = acc + shifted(off)
        done = half
        cnt = jnp.minimum(t + half, seq_len) - jnp.maximum(t - half, 0)
        pooled = jnp.where(lane_group == g, acc / cnt.astype(F32), pooled)
    o_ref[...] = _dot(pooled - x, w_ref[...]) * s_ref[...]


def _pool_call(px, w_pool_bd, pool_scale, batch, seq_len):
    t = batch * seq_len
    return pl.pallas_call(
        functools.partial(_pool_kernel, seq_len=seq_len),
        out_shape=jax.ShapeDtypeStruct((t, 256), F32),
        grid=(batch,),
        in_specs=[
            pl.BlockSpec((seq_len, 256), lambda b: (b, 0)),
            pl.BlockSpec((256, 256), lambda b: (0, 0)),
            pl.BlockSpec((1, 256), lambda b: (0, 0)),
        ],
        out_specs=pl.BlockSpec((seq_len, 256), lambda b: (b, 0)),
        scratch_shapes=[pltpu.VMEM((seq_len + 2 * POOL_PAD, GROUP_WIDTH), F32)],
        compiler_params=_params("parallel"),
        name="multiscale_pool",
    )(px, w_pool_bd, pool_scale)


def _outproj_kernel(a_ref, f_ref, n_ref, p_ref, x_ref, ada_ref, w_ref, g_ref, b_ref, o_ref):
    mix = (_dot(a_ref[...], w_ref[0:256, :]) + _dot(f_ref[...], w_ref[256:512, :])
           + _dot(n_ref[...], w_ref[512:768, :]) + _dot(p_ref[...], w_ref[768:1024, :]))
    gate = ada_ref[:, 2 * D_MODEL:3 * D_MODEL]
    z = ALPHA * x_ref[...] + gate * mix
    o_ref[...] = _layer_norm(z) * g_ref[...] + b_ref[...]


def _outproj_call(a_out, f_out, n_out, p_out, x, ada, cond_base, cond_span, w_out, ln_g, ln_b):
    t = x.shape[0]
    tm = TOKEN_TILE
    tiles_per_cond = cond_span // tm
    part = pl.BlockSpec((tm, 256), lambda i: (i, 0))
    row = pl.BlockSpec((1, D_MODEL), lambda i: (0, 0))
    return pl.pallas_call(
        _outproj_kernel,
        out_shape=jax.ShapeDtypeStruct((t, D_MODEL), F32),
        grid=(t // tm,),
        in_specs=[part, part, part, part,
                  pl.BlockSpec((tm, D_MODEL), lambda i: (i, 0)),
                  pl.BlockSpec((None, 1, ADA_CHUNKS * D_MODEL),
                               lambda i: (cond_base + i // tiles_per_cond, 0, 0)),
                  pl.BlockSpec((D_MODEL, D_MODEL), lambda i: (0, 0)), row, row],
        out_specs=pl.BlockSpec((tm, D_MODEL), lambda i: (i, 0)),
        compiler_params=_params("parallel"),
        name="outproj_postnorm",
    )(a_out, f_out, n_out, p_out, x, ada, w_out, ln_g, ln_b)


def _sorting_network(n):
    pairs = []
    p = 1
    while p < n:
        k = p
        while k >= 1:
            for j in range(k % p, n - k, 2 * k):
                for i in range(min(k, n - j - k)):
                    if (i + j) // (2 * p) == (i + j + k) // (2 * p):
                        pairs.append((i + j, i + j + k))
            k //= 2
        p *= 2
    return pairs


def _sort_descending(rows):
    rows = list(rows)
    for a, b in _sorting_network(len(rows)):
        rows[a], rows[b] = jnp.maximum(rows[a], rows[b]), jnp.minimum(rows[a], rows[b])
    return rows


def _sublane_max_all(x):
    x = jnp.maximum(x, pltpu.roll(x, 4, axis=0))
    x = jnp.maximum(x, pltpu.roll(x, 2, axis=0))
    return jnp.maximum(x, pltpu.roll(x, 1, axis=0))


def _extract_top(groups, count):
    groups = [list(g) for g in groups]
    out = []
    for it in range(count):
        head = groups[0][0]
        for g in groups[1:]:
            head = jnp.maximum(head, g[0])
        m = _sublane_max_all(head)
        out.append(m)
        remaining = count - it - 1
        if remaining == 0:
            break
        for g in groups:
            eq = g[0] == m
            keep = min(len(g), remaining)
            for i in range(keep):
                nxt = g[i + 1] if i + 1 < len(g) else jnp.full_like(m, -jnp.inf)
                g[i] = jnp.where(eq, nxt, g[i])
            del g[keep:]
    return out


def _peer_route(s1, s2):
    groups = PEER_N_KEYS // 8
    s1_rows = [s1[8 * i:8 * (i + 1)] for i in range(groups)]
    s2_rows = [s2[8 * i:8 * (i + 1)] for i in range(groups)]
    v1 = _extract_top([_sort_descending(s1_rows)], PEER_TOPK + 1)
    v2 = _extract_top([_sort_descending(s2_rows)], PEER_TOPK + 1)
    sub = lax.broadcasted_iota(jnp.int32, v1[0].shape, 0)
    lo = v1[0]
    hi = v1[8]
    for r in range(1, 8):
        lo = jnp.where(sub == r, v1[r], lo)
        hi = jnp.where(sub == r, v1[8 + r], hi)
    cand = [[lo + v2[b] for b in range(PEER_TOPK + 1)], [hi + v2[0]], [v1[PEER_TOPK] + v2[0]]]
    top = _extract_top(cand, PEER_TOPK + 1)
    den = jnp.ones_like(top[0])
    for tk in top[1:PEER_TOPK]:
        den = den + jnp.exp(tk - top[0])
    inv_den = 1.0 / den
    tau = 0.5 * (top[PEER_TOPK - 1] + top[PEER_TOPK])
    thr = [tau - r for r in s1_rows]
    coef = [jnp.exp(r - v1[0]) * inv_den for r in s1_rows]
    e2 = [jnp.exp(r - v2[0]) for r in s2_rows]
    return thr, coef, e2


def _peer_kernel(x_ref, ada_ref, wq_ref, k1_ref, k2_ref, u_ref, vt_ref, g_ref, b_ref, o_ref,
                 h_ref, ht_ref, yt_ref, act_ref, s2_ref, e2_ref, thr_ref, coef_ref):
    j = pl.program_id(1)
    tt = x_ref.shape[0]
    half = PEER_KEY_DIM // 2
    lanes = PEER_LANE_CHUNK

    @pl.when(j == 0)
    def _():
        sh = ada_ref[:, 3 * D_MODEL:4 * D_MODEL]
        sc = ada_ref[:, 4 * D_MODEL:5 * D_MODEL]
        h = _layer_norm(x_ref[...]) * (1.0 + sc) + sh
        h_ref[...] = h.astype(BF16)
        ht_ref[...] = h.T.astype(BF16)
        yt_ref[...] = jnp.zeros_like(yt_ref)

        def one_head(hd, carry):
            w_rows = wq_ref[pl.ds(pl.multiple_of(hd * PEER_KEY_DIM, PEER_KEY_DIM), PEER_KEY_DIM), :]
            q_t = lax.dot_general(w_rows, h_ref[...], (((1,), (1,)), ((), ())),
                                  preferred_element_type=F32)
            s1 = _dot(k1_ref[...], q_t[0:half])
            s2 = _dot(k2_ref[...], q_t[half:2 * half])
            for c in range(tt // lanes):
                cols = slice(c * lanes, (c + 1) * lanes)
                s2_ref[hd, c] = s2[:, cols]
                thr, coef, e2 = _peer_route(s1[:, cols], s2[:, cols])
                for i in range(PEER_N_KEYS // 8):
                    rows = pl.ds(8 * i, 8)
                    thr_ref[hd, c, rows, :] = thr[i]
                    coef_ref[hd, c, rows, :] = coef[i]
                    e2_ref[hd, c, rows, :] = e2[i]
            return carry

        lax.fori_loop(0, PEER_HEADS, one_head, 0)

    keys_per_block = PEER_EXPERT_TILE // PEER_N_KEYS
    keys_per_sub = PEER_SUB_TILE // PEER_N_KEYS
    n_sub = PEER_EXPERT_TILE // PEER_SUB_TILE

    def up_proj(sub):
        return jnp.dot(u_ref[sub * PEER_SUB_TILE:(sub + 1) * PEER_SUB_TILE, :], ht_ref[...],
                       preferred_element_type=F32)

    acc = None
    a_next = up_proj(0)
    for sub in range(n_sub):
        sub_rows = slice(sub * PEER_SUB_TILE, (sub + 1) * PEER_SUB_TILE)
        a_sub = a_next
        if sub + 1 < n_sub:
            a_next = up_proj(sub + 1)
        for ii in range(keys_per_sub):
            i1 = j * keys_per_block + sub * keys_per_sub + ii
            rows = slice(ii * PEER_N_KEYS, (ii + 1) * PEER_N_KEYS)
            for c in range(tt // lanes):
                cols = slice(c * lanes, (c + 1) * lanes)
                w = None
                for hd in range(PEER_HEADS):
                    thr = thr_ref[hd, c, pl.ds(i1, 1), :]
                    coef = coef_ref[hd, c, pl.ds(i1, 1), :]
                    term = jnp.where(s2_ref[hd, c] >= thr, e2_ref[hd, c], 0.0) * coef
                    w = term if w is None else w + term
                a = a_sub[rows, cols]
                gelu = 0.5 * a * (1.0 + lax.erf(a * (2.0 ** -0.5)))
                act_ref[sub * PEER_SUB_TILE + ii * PEER_N_KEYS:
                        sub * PEER_SUB_TILE + (ii + 1) * PEER_N_KEYS, cols] = (w * gelu).astype(BF16)
        part = jnp.dot(vt_ref[:, sub_rows], act_ref[sub_rows, :], preferred_element_type=F32)
        acc = part if acc is None else acc + part
    yt_ref[...] += acc

    @pl.when(j == pl.num_programs(1) - 1)
    def _():
        gate = ada_ref[:, 5 * D_MODEL:6 * D_MODEL]
        z = ALPHA * x_ref[...] + gate * yt_ref[...].T
        o_ref[...] = _layer_norm(z) * g_ref[...] + b_ref[...]


def _peer_call(x, ada, cond_base, cond_span, layer, wq_t, k1, k2, u, v_t, ln_g, ln_b):
    t = x.shape[0]
    tt = PEER_TOKEN_TILE
    tiles_per_cond = cond_span // tt
    eb = PEER_EXPERT_TILE
    row = pl.BlockSpec((1, D_MODEL), lambda i, j: (0, 0))
    per_head = pltpu.VMEM((PEER_HEADS, tt // PEER_LANE_CHUNK, PEER_N_KEYS, PEER_LANE_CHUNK), F32)
    return pl.pallas_call(
        _peer_kernel,
        out_shape=jax.ShapeDtypeStruct((t, D_MODEL), F32),
        grid=(t // tt, PEER_N_EXPERTS // eb),
        in_specs=[
            pl.BlockSpec((tt, D_MODEL), lambda i, j: (i, 0)),
            pl.BlockSpec((None, 1, ADA_CHUNKS * D_MODEL),
                         lambda i, j: (cond_base + i // tiles_per_cond, 0, 0)),
            pl.BlockSpec((PEER_HEADS * PEER_KEY_DIM, D_MODEL), lambda i, j: (0, 0)),
            pl.BlockSpec((PEER_N_KEYS, PEER_KEY_DIM // 2), lambda i, j: (0, 0)),
            pl.BlockSpec((PEER_N_KEYS, PEER_KEY_DIM // 2), lambda i, j: (0, 0)),
            pl.BlockSpec((None, eb, D_MODEL), lambda i, j: (layer, j, 0)),
            pl.BlockSpec((None, D_MODEL, eb), lambda i, j: (layer, 0, j)),
            row, row,
        ],
        out_specs=pl.BlockSpec((tt, D_MODEL), lambda i, j: (i, 0)),
        scratch_shapes=[
            pltpu.VMEM((tt, D_MODEL), BF16),
            pltpu.VMEM((D_MODEL, tt), BF16),
            pltpu.VMEM((D_MODEL, tt), F32),
            pltpu.VMEM((eb, tt), BF16),
            per_head, per_head, per_head, per_head,
        ],
        compiler_params=_params("parallel", "arbitrary"),
        name="peer",
    )(x, ada, wq_t, k1, k2, u, v_t, ln_g, ln_b)


def _rope_tables(seq_len):
    t = np.arange(seq_len)
    row = (t // GRID_W).astype(np.float64)
    col = (t % GRID_W).astype(np.float64)
    nf = HEAD_DIM // 4
    inv = ROPE_THETA ** (-np.arange(nf, dtype=np.float64) / nf)
    d = np.arange(HEAD_DIM)
    pos = np.where(d[None, :] < HEAD_DIM // 2, row[:, None], col[:, None])
    ang = pos * inv[d % nf][None, :]
    sign = np.where((d % (HEAD_DIM // 2)) < nf, -1.0, 1.0)
    cos = np.tile(np.cos(ang), (1, ATTN_HEADS))
    sin = np.tile(np.sin(ang) * sign[None, :], (1, ATTN_HEADS))
    return jnp.asarray(cos, F32), jnp.asarray(sin, F32)


def _head_mean_matrix():
    m = np.kron(np.eye(ATTN_HEADS), np.full((HEAD_DIM, HEAD_DIM), 1.0 / HEAD_DIM))
    return jnp.asarray(m, BF16)


def _block_diag(w):
    g, a, b = w.shape
    out = jnp.zeros((g * a, g * b), w.dtype)
    for i in range(g):
        out = out.at[i * a:(i + 1) * a, i * b:(i + 1) * b].set(w[i])
    return out


def _trunk_layer(x, ada, cond_base, batch, seq_len, lw, ctx_cache):
    rope = ctx_cache is not None
    if rope:
        cos_t, sin_t = _rope_tables(seq_len)
    else:
        cos_t = jnp.ones((seq_len, 256), F32)
        sin_t = jnp.zeros((seq_len, 256), F32)
    cond_span = seq_len if rope else batch * seq_len
    aq, ak, av, fx, nq, nk, nv, px = _inproj_call(
        x, ada, cond_base, cond_span, seq_len, lw["w_in"], lw["gq"], lw["gk"], lw["head_mean"],
        cos_t, sin_t, rope)
    if ctx_cache is None:
        a_out, n_out = _ctx_attn_call(aq, ak, av, nq, nk, nv, batch, seq_len)
    else:
        ck_a, cv_a, ck_n, cv_n = ctx_cache
        a_out = _lat_attn_call(aq, ak, av, ck_a, cv_a, batch, seq_len)
        n_out = _na_call(nq, nk, nv, ck_n, cv_n, lw["na_bias"], batch, seq_len)
    f_out = _fourier_call(fx, lw["w_fourier"], batch, seq_len)
    p_out = _pool_call(px, lw["w_pool"], lw["pool_scale"], batch, seq_len)
    x = _outproj_call(a_out, f_out, n_out, p_out, x, ada, cond_base, cond_span,
                      lw["w_out"], lw["ln1_g"], lw["ln1_b"])
    x = _peer_call(x, ada, cond_base, cond_span, lw["layer"], lw["wq_t"], lw["k1"], lw["k2"],
                   lw["u"], lw["v_t"], lw["ln2_g"], lw["ln2_b"])
    return x, (ak, av, nk, nv)


def kernel(x_prompt, x_sample, cache_attn_k, cache_attn_v, cache_na_k, cache_na_v, c, c_ctx,
           w_ada, b_ada, w_in, q_norm_g, k_norm_g, w_fourier, na_bias, w_pool, pool_scale,
           w_out, ln1_g, ln1_b, ln2_g, ln2_b, peer_wq, peer_k1, peer_k2, peer_u, peer_v):
    batch, seq, d = x_prompt.shape
    dec_batch, dec_seq, _ = x_sample.shape
    xp = x_prompt.reshape(batch * seq, d)
    xs = x_sample.reshape(dec_batch * dec_seq, d)

    cond = jnp.zeros((COND_ROWS, d), F32).at[0].set(c_ctx).at[1:1 + dec_batch].set(c)
    ada_all = _ada_call(cond, w_ada, b_ada)
    head_mean = _head_mean_matrix()
    rows = dec_seq // GRID_W

    def heads_first(cache):
        return cache.transpose(1, 0, 3, 2, 4)

    cak, cav = heads_first(cache_attn_k), heads_first(cache_attn_v)
    cnk, cnv = heads_first(cache_na_k), heads_first(cache_na_v)

    u_all = peer_u.astype(BF16)
    v_t_all = jnp.swapaxes(peer_v.astype(BF16), 1, 2)

    new_kv = [[], [], [], []]
    for l in range(DEPTH):
        lw = {
            "layer": l,
            "w_in": w_in[l].astype(BF16),
            "gq": jnp.tile(q_norm_g[l], ATTN_HEADS).reshape(1, -1),
            "gk": jnp.tile(k_norm_g[l], ATTN_KV_HEADS).reshape(1, -1),
            "head_mean": head_mean,
            "w_fourier": w_fourier[l].astype(BF16),
            "na_bias": _na_bias_patterns(na_bias[l], rows),
            "w_pool": _block_diag(w_pool[l]).astype(BF16),
            "pool_scale": pool_scale[l].reshape(1, -1),
            "w_out": w_out[l].astype(BF16),
            "ln1_g": ln1_g[l].reshape(1, -1), "ln1_b": ln1_b[l].reshape(1, -1),
            "ln2_g": ln2_g[l].reshape(1, -1), "ln2_b": ln2_b[l].reshape(1, -1),
            "wq_t": peer_wq[l].T.astype(BF16),
            "k1": peer_k1[l].astype(BF16), "k2": peer_k2[l].astype(BF16),
            "u": u_all,
            "v_t": v_t_all,
        }
        ada = ada_all[l].reshape(COND_ROWS, 1, ADA_CHUNKS * d)
        xp, kv = _trunk_layer(xp, ada, 0, batch, seq, lw, None)
        for dst, arr in zip(new_kv, kv):
            heads = arr.shape[0]
            dst.append(arr.reshape(heads, batch, seq, HEAD_DIM).transpose(1, 2, 0, 3))
        xs, _ = _trunk_layer(xs, ada, 1, dec_batch, dec_seq, lw,
                             (cak[l], cav[l], cnk[l], cnv[l]))
    outs = [jnp.stack(lst, axis=1) for lst in new_kv]
    return (xp.reshape(batch, seq, d), xs.reshape(dec_batch, dec_seq, d), *outs)
```

```python
import functools
import math

import numpy as np
import jax
import jax.numpy as jnp
from jax import lax
from jax.experimental import pallas as pl
from jax.experimental.pallas import tpu as pltpu

D_MODEL = 1024
DEPTH = 2
GRID_W = 64
HEAD_DIM = 64
GROUP_WIDTH = 256
ATTN_HEADS = 4
ATTN_KV_HEADS = 2
NA_HEADS = 4
NA_KH = 8
NA_KW = 16
FOURIER_HEADS = 4
POOL_WINDOWS = (2, 4, 8, 16)
POOL_GROUP_W = GROUP_WIDTH // len(POOL_WINDOWS)
ROPE_THETA = 10000.0
ATTN_SCALE = HEAD_DIM ** -0.5
PEER_HEADS = 8
PEER_N_KEYS = 128
PEER_N_EXPERTS = PEER_N_KEYS * PEER_N_KEYS
PEER_KEY_DIM = 256
PEER_TOPK = 16
ALPHA = (2 * DEPTH) ** 0.25
LN_EPS = 1e-6
RMS_EPS = 1e-6
ADA_CHUNKS = 6
P_IN = 1792
COND_ROWS = 8

TOKEN_TILE = 256
ATTN_Q_TILE = 512
DFT_ROW_TILE = 512
NA_ROWS_PER_ITER = 4
PEER_TOKEN_TILE = 512
PEER_EXPERT_TILE = 2048
PEER_SUB_TILE = 256
PEER_LANE_CHUNK = 128
ADA_COL_TILE = 1024
NEG_BIG = -1e30
VMEM_LIMIT = 56 * 1024 * 1024

F32 = jnp.float32
BF16 = jnp.bfloat16


def _params(*sem):
    return pltpu.CompilerParams(dimension_semantics=sem, vmem_limit_bytes=VMEM_LIMIT)


def _dot(a, b):
    return jnp.dot(a.astype(BF16), b.astype(BF16), preferred_element_type=F32)


def _dot_nt(a, b):
    return lax.dot_general(a.astype(BF16), b.astype(BF16), (((1,), (1,)), ((), ())),
                           preferred_element_type=F32)


def _layer_norm(x):
    mu = jnp.mean(x, axis=-1, keepdims=True)
    xc = x - mu
    var = jnp.mean(xc * xc, axis=-1, keepdims=True)
    return xc * lax.rsqrt(var + LN_EPS)


def _ada_kernel(cond_ref, w_ref, b_ref, o_ref):
    c = cond_ref[...]
    s = c / (1.0 + jnp.exp(-c))
    o_ref[...] = jnp.dot(s, w_ref[...], precision=lax.Precision.HIGHEST,
                         preferred_element_type=F32) + b_ref[...]


def _ada_call(cond, w_ada, b_ada):
    n = ADA_CHUNKS * D_MODEL
    return pl.pallas_call(
        _ada_kernel,
        out_shape=jax.ShapeDtypeStruct((DEPTH, COND_ROWS, n), F32),
        grid=(DEPTH, n // ADA_COL_TILE),
        in_specs=[
            pl.BlockSpec((COND_ROWS, D_MODEL), lambda l, j: (0, 0)),
            pl.BlockSpec((None, D_MODEL, ADA_COL_TILE), lambda l, j: (l, 0, j)),
            pl.BlockSpec((None, 1, ADA_COL_TILE), lambda l, j: (l, 0, j)),
        ],
        out_specs=pl.BlockSpec((None, COND_ROWS, ADA_COL_TILE), lambda l, j: (l, 0, j)),
        compiler_params=_params("parallel", "arbitrary"),
        name="ada_proj",
    )(cond, w_ada, b_ada.reshape(DEPTH, 1, n))


def _head_mean_sq(x, g):
    sq = x * x
    hi = sq.astype(BF16)
    lo = (sq - hi.astype(F32)).astype(BF16)
    return (jnp.dot(hi, g, preferred_element_type=F32)
            + jnp.dot(lo, g, preferred_element_type=F32))


def _rotary(x, cos, sin_signed):
    w = x.shape[-1]
    fwd = pltpu.roll(x, w - 16, axis=1)
    bwd = pltpu.roll(x, 16, axis=1)
    lane = lax.broadcasted_iota(jnp.int32, x.shape, 1)
    partner = jnp.where((lane % 32) < 16, fwd, bwd)
    return x * cos + partner * sin_signed


def _inproj_kernel(x_ref, ada_ref, w_ref, gq_ref, gk_ref, hm_ref, cos_ref, sin_ref,
                   aq_ref, ak_ref, av_ref, fx_ref, nq_ref, nk_ref, nv_ref, px_ref, *, rope):
    x = x_ref[...]
    sh = ada_ref[:, 0:D_MODEL]
    sc = ada_ref[:, D_MODEL:2 * D_MODEL]
    h = _layer_norm(x) * (1.0 + sc) + sh
    proj = _dot(h, w_ref[...])
    aq = proj[:, 0:256]
    ak = proj[:, 256:384]
    hm = hm_ref[...]
    aq = aq * lax.rsqrt(_head_mean_sq(aq, hm) + RMS_EPS) * gq_ref[...]
    ak = ak * lax.rsqrt(_head_mean_sq(ak, hm[0:128, 0:128]) + RMS_EPS) * gk_ref[...]
    if rope:
        cos = cos_ref[...]
        sin = sin_ref[...]
        aq = _rotary(aq, cos, sin)
        ak = _rotary(ak, cos[:, 0:128], sin[:, 0:128])
    for hd in range(ATTN_HEADS):
        aq_ref[hd] = aq[:, hd * 64:(hd + 1) * 64]
    for hd in range(ATTN_KV_HEADS):
        ak_ref[hd] = ak[:, hd * 64:(hd + 1) * 64]
        av_ref[hd] = proj[:, 384 + hd * 64:384 + (hd + 1) * 64]
    fx_ref[...] = proj[:, 512:768]
    for hd in range(NA_HEADS):
        nq_ref[hd] = proj[:, 768 + hd * 64:768 + (hd + 1) * 64]
        nk_ref[hd] = proj[:, 1024 + hd * 64:1024 + (hd + 1) * 64]
        nv_ref[hd] = proj[:, 1280 + hd * 64:1280 + (hd + 1) * 64]
    px_ref[...] = proj[:, 1536:1792]


def _inproj_call(x, ada, cond_base, cond_span, seq_len, w_in, gq, gk, head_mean, cos_t, sin_t,
                 rope):
    t = x.shape[0]
    tm = TOKEN_TILE
    tiles_per_seq = seq_len // tm
    tiles_per_cond = cond_span // tm
    hm = lambda n: jax.ShapeDtypeStruct((n, t, HEAD_DIM), F32)
    hspec = lambda n: pl.BlockSpec((n, tm, HEAD_DIM), lambda i: (0, i, 0))
    const = lambda shape: pl.BlockSpec(shape, lambda i: tuple(0 for _ in shape))
    return pl.pallas_call(
        functools.partial(_inproj_kernel, rope=rope),
        out_shape=(hm(4), hm(2), hm(2), jax.ShapeDtypeStruct((t, 256), F32),
                   hm(4), hm(4), hm(4), jax.ShapeDtypeStruct((t, 256), F32)),
        grid=(t // tm,),
        in_specs=[
            pl.BlockSpec((tm, D_MODEL), lambda i: (i, 0)),
            pl.BlockSpec((None, 1, ADA_CHUNKS * D_MODEL),
                         lambda i: (cond_base + i // tiles_per_cond, 0, 0)),
            const((D_MODEL, P_IN)),
            const((1, 256)), const((1, 128)), const((256, 256)),
            pl.BlockSpec((tm, 256), lambda i: (i % tiles_per_seq, 0)),
            pl.BlockSpec((tm, 256), lambda i: (i % tiles_per_seq, 0)),
        ],
        out_specs=(hspec(4), hspec(2), hspec(2), pl.BlockSpec((tm, 256), lambda i: (i, 0)),
                   hspec(4), hspec(4), hspec(4), pl.BlockSpec((tm, 256), lambda i: (i, 0))),
        compiler_params=_params("parallel"),
        name="inproj_rope" if rope else "inproj",
    )(x, ada, w_in, gq, gk, head_mean, cos_t, sin_t)


def _softmax_attend(q, k, v):
    s = _dot_nt(q, k) * ATTN_SCALE
    m = jnp.max(s, axis=-1, keepdims=True)
    p = jnp.exp(s - m)
    return _dot(p, v) / jnp.sum(p, axis=-1, keepdims=True)


def _ctx_attn_kernel(aq_ref, ak_ref, av_ref, nq_ref, nk_ref, nv_ref, a_ref, n_ref):
    outs = [_softmax_attend(aq_ref[hd], ak_ref[hd // 2], av_ref[hd // 2])
            for hd in range(ATTN_HEADS)]
    a_ref[...] = jnp.concatenate(outs, axis=-1)
    outs = [_softmax_attend(nq_ref[hd], nk_ref[hd], nv_ref[hd]) for hd in range(NA_HEADS)]
    n_ref[...] = jnp.concatenate(outs, axis=-1)


def _ctx_attn_call(aq, ak, av, nq, nk, nv, batch, seq_len):
    t = batch * seq_len
    hspec = lambda n: pl.BlockSpec((n, seq_len, HEAD_DIM), lambda b: (0, b, 0))
    ospec = pl.BlockSpec((seq_len, 256), lambda b: (b, 0))
    return pl.pallas_call(
        _ctx_attn_kernel,
        out_shape=(jax.ShapeDtypeStruct((t, 256), F32), jax.ShapeDtypeStruct((t, 256), F32)),
        grid=(batch,),
        in_specs=[hspec(4), hspec(2), hspec(2), hspec(4), hspec(4), hspec(4)],
        out_specs=(ospec, ospec),
        compiler_params=_params("parallel"),
        name="ctx_attention",
    )(aq, ak, av, nq, nk, nv)


def _lat_attn_kernel(q_ref, k_ref, v_ref, kc_ref, vc_ref, o_ref):
    k = k_ref[...]
    v = v_ref[...]
    kc = kc_ref[...]
    vc = vc_ref[...]
    outs = []
    for g in range(2):
        q = q_ref[g]
        s_l = _dot_nt(q, k) * ATTN_SCALE
        s_c = _dot_nt(q, kc) * ATTN_SCALE
        m = jnp.maximum(jnp.max(s_l, axis=-1, keepdims=True), jnp.max(s_c, axis=-1, keepdims=True))
        p_l = jnp.exp(s_l - m)
        p_c = jnp.exp(s_c - m)
        den = jnp.sum(p_l, axis=-1, keepdims=True) + jnp.sum(p_c, axis=-1, keepdims=True)
        outs.append((_dot(p_l, v) + _dot(p_c, vc)) / den)
    o_ref[...] = jnp.concatenate(outs, axis=-1)


def _lat_attn_call(aq, ak, av, kc, vc, batch, seq_len):
    t = batch * seq_len
    tq = ATTN_Q_TILE
    nqb = seq_len // tq
    return pl.pallas_call(
        _lat_attn_kernel,
        out_shape=jax.ShapeDtypeStruct((t, 256), F32),
        grid=(batch, ATTN_KV_HEADS, nqb),
        in_specs=[
            pl.BlockSpec((2, tq, HEAD_DIM), lambda b, g, i: (g, b * nqb + i, 0)),
            pl.BlockSpec((None, seq_len, HEAD_DIM), lambda b, g, i: (g, b, 0)),
            pl.BlockSpec((None, seq_len, HEAD_DIM), lambda b, g, i: (g, b, 0)),
            pl.BlockSpec((None, None, kc.shape[2], HEAD_DIM), lambda b, g, i: (b, g, 0, 0)),
            pl.BlockSpec((None, None, kc.shape[2], HEAD_DIM), lambda b, g, i: (b, g, 0, 0)),
        ],
        out_specs=pl.BlockSpec((tq, 128), lambda b, g, i: (b * nqb + i, g)),
        compiler_params=_params("parallel", "parallel", "arbitrary"),
        name="lat_attention",
    )(aq, ak, av, kc, vc)


def _na_kernel(q_ref, k_ref, v_ref, kc_ref, vc_ref, bias_ref, o_ref, *, rows):
    kh = min(NA_KH, rows)

    def one_row(r):
        rs = jnp.clip(r - kh // 2, 0, rows - kh)
        q_off = pl.multiple_of(r * GRID_W, GRID_W)
        w_off = pl.multiple_of(rs * GRID_W, GRID_W)
        outs = []
        for hd in range(2):
            q = q_ref[hd, pl.ds(q_off, GRID_W), :]
            kw = k_ref[hd, pl.ds(w_off, kh * GRID_W), :]
            vw = v_ref[hd, pl.ds(w_off, kh * GRID_W), :]
            s_w = _dot_nt(q, kw) * ATTN_SCALE + bias_ref[hd, r - rs]
            s_c = _dot_nt(q, kc_ref[hd]) * ATTN_SCALE
            m = jnp.maximum(jnp.max(s_w, axis=-1, keepdims=True),
                            jnp.max(s_c, axis=-1, keepdims=True))
            p_w = jnp.exp(s_w - m)
            p_c = jnp.exp(s_c - m)
            den = jnp.sum(p_w, axis=-1, keepdims=True) + jnp.sum(p_c, axis=-1, keepdims=True)
            outs.append((_dot(p_w, vw) + _dot(p_c, vc_ref[hd])) / den)
        o_ref[pl.ds(q_off, GRID_W), :] = jnp.concatenate(outs, axis=-1)

    def row_group(g, carry):
        for k in range(NA_ROWS_PER_ITER):
            one_row(g * NA_ROWS_PER_ITER + k)
        return carry

    lax.fori_loop(0, rows // NA_ROWS_PER_ITER, row_group, 0)


def _na_call(nq, nk, nv, kc, vc, bias, batch, seq_len):
    t = batch * seq_len
    rows = seq_len // GRID_W
    lspec = pl.BlockSpec((2, seq_len, HEAD_DIM), lambda b, hp: (hp, b, 0))
    cspec = pl.BlockSpec((None, 2, kc.shape[2], HEAD_DIM), lambda b, hp: (b, hp, 0, 0))
    return pl.pallas_call(
        functools.partial(_na_kernel, rows=rows),
        out_shape=jax.ShapeDtypeStruct((t, 256), F32),
        grid=(batch, NA_HEADS // 2),
        in_specs=[lspec, lspec, lspec, cspec, cspec,
                  pl.BlockSpec((2,) + bias.shape[1:], lambda b, hp: (hp, 0, 0, 0))],
        out_specs=pl.BlockSpec((seq_len, 128), lambda b, hp: (b, hp)),
        compiler_params=_params("parallel", "parallel"),
        name="neighbourhood_attention",
    )(nq, nk, nv, kc, vc, bias)


def _na_bias_patterns(bias_table, rows):
    kh = min(NA_KH, rows)
    c = np.arange(GRID_W)
    cs = np.clip(c - NA_KW // 2, 0, GRID_W - NA_KW)
    kc = np.arange(GRID_W)
    inside = (kc[None, :] >= cs[:, None]) & (kc[None, :] < cs[:, None] + NA_KW)
    col_rel = np.clip(kc[None, :] - c[:, None] + (NA_KW - 1), 0, 2 * NA_KW - 2)
    pi = np.arange(kh)
    k = np.arange(kh)
    row_rel = np.clip(k[None, :] + (NA_KH - 1) - pi[:, None], 0, 2 * NA_KH - 2)
    row_pick = (row_rel[:, :, None] == np.arange(2 * NA_KH - 1)).astype(np.float32)
    col_pick = (col_rel[:, :, None] == np.arange(2 * NA_KW - 1)).astype(np.float32)
    b = jnp.einsum("hrs,cqs->hrcq", bias_table, col_pick, precision=lax.Precision.HIGHEST)
    b = jnp.einsum("pkr,hrcq->hpckq", row_pick, b, precision=lax.Precision.HIGHEST)
    b = jnp.where(inside[None, None, :, None, :], b, NEG_BIG)
    return b.reshape(bias_table.shape[0], kh, GRID_W, kh * GRID_W)


def _fourier_kernel(x_ref, cl_ref, sl_ref, cc_ref, sc_ref, w_ref, o_ref, *, norm):
    x = x_ref[...]
    xc = _dot(x, cc_ref[...])
    xs = _dot(x, sc_ref[...])
    y = (_dot(cl_ref[...], xc) - _dot(sl_ref[...], xs)) * norm
    o_ref[...] = _dot(y, w_ref[...])


def _fourier_call(fx, w_fourier, batch, seq_len):
    t = batch * seq_len
    tm = min(DFT_ROW_TILE, seq_len)
    nmb = seq_len // tm
    j = np.arange(seq_len, dtype=np.float64)
    ang_l = 2.0 * np.pi * ((j[:, None] * j[None, :]) % seq_len) / seq_len
    hw = GROUP_WIDTH // FOURIER_HEADS
    c = np.arange(hw, dtype=np.float64)
    ang_c = 2.0 * np.pi * ((c[:, None] * c[None, :]) % hw) / hw
    eye = np.eye(FOURIER_HEADS)
    cl = jnp.asarray(np.cos(ang_l), F32)
    sl = jnp.asarray(np.sin(ang_l), F32)
    cc = jnp.asarray(np.kron(eye, np.cos(ang_c)), F32)
    sc = jnp.asarray(np.kron(eye, np.sin(ang_c)), F32)
    const = lambda shape: pl.BlockSpec(shape, lambda i, b: (0, 0))
    return pl.pallas_call(
        functools.partial(_fourier_kernel, norm=float((seq_len * hw) ** -0.5)),
        out_shape=jax.ShapeDtypeStruct((t, 256), F32),
        grid=(nmb, batch),
        in_specs=[
            pl.BlockSpec((seq_len, 256), lambda i, b: (b, 0)),
            pl.BlockSpec((tm, seq_len), lambda i, b: (i, 0)),
            pl.BlockSpec((tm, seq_len), lambda i, b: (i, 0)),
            const((256, 256)), const((256, 256)), const((256, 256)),
        ],
        out_specs=pl.BlockSpec((tm, 256), lambda i, b: (b * nmb + i, 0)),
        compiler_params=_params("parallel", "arbitrary"),
        name="fourier_mix",
    )(fx, cl, sl, cc, sc, w_fourier)


POOL_PAD = 8


def _pool_kernel(x_ref, w_ref, s_ref, o_ref, pad_ref, *, seq_len):
    x = x_ref[...]
    zeros = jnp.zeros((POOL_PAD, GROUP_WIDTH), F32)
    pad_ref[0:POOL_PAD, :] = zeros
    pad_ref[POOL_PAD + seq_len:2 * POOL_PAD + seq_len, :] = zeros
    pad_ref[POOL_PAD:POOL_PAD + seq_len, :] = x

    def shifted(off):
        return pad_ref[POOL_PAD + off:POOL_PAD + off + seq_len, :]

    t = lax.broadcasted_iota(jnp.int32, (seq_len, GROUP_WIDTH), 0)
    lane_group = lax.broadcasted_iota(jnp.int32, (seq_len, GROUP_WIDTH), 1) // POOL_GROUP_W
    acc = x
    done = 0
    pooled = jnp.zeros_like(x)
    for g, w in enumerate(POOL_WINDOWS):
        half = w // 2
        for off in list(range(-half, -done)) + list(range(max(done, 1), half)):
            acc = acc + shifted(off)
        done = half
        cnt = jnp.minimum(t + half, seq_len) - jnp.maximum(t - half, 0)
        pooled = jnp.where(lane_group == g, acc / cnt.astype(F32), pooled)
    o_ref[...] = _dot(pooled - x, w_ref[...]) * s_ref[...]


def _pool_call(px, w_pool_bd, pool_scale, batch, seq_len):
    t = batch * seq_len
    return pl.pallas_call(
        functools.partial(_pool_kernel, seq_len=seq_len),
        out_shape=jax.ShapeDtypeStruct((t, 256), F32),
        grid=(batch,),
        in_specs=[
            pl.BlockSpec((seq_len, 256), lambda b: (b, 0)),
            pl.BlockSpec((256, 256), lambda b: (0, 0)),
            pl.BlockSpec((1, 256), lambda b: (0, 0)),
        ],
        out_specs=pl.BlockSpec((seq_len, 256), lambda b: (b, 0)),
        scratch_shapes=[pltpu.VMEM((seq_len + 2 * POOL_PAD, GROUP_WIDTH), F32)],
        compiler_params=_params("parallel"),
        name="multiscale_pool",
    )(px, w_pool_bd, pool_scale)


def _outproj_kernel(a_ref, f_ref, n_ref, p_ref, x_ref, ada_ref, w_ref, g_ref, b_ref, o_ref):
    mix = (_dot(a_ref[...], w_ref[0:256, :]) + _dot(f_ref[...], w_ref[256:512, :])
           + _dot(n_ref[...], w_ref[512:768, :]) + _dot(p_ref[...], w_ref[768:1024, :]))
    gate = ada_ref[:, 2 * D_MODEL:3 * D_MODEL]
    z = ALPHA * x_ref[...] + gate * mix
    o_ref[...] = _layer_norm(z) * g_ref[...] + b_ref[...]


def _outproj_call(a_out, f_out, n_out, p_out, x, ada, cond_base, cond_span, w_out, ln_g, ln_b):
    t = x.shape[0]
    tm = TOKEN_TILE
    tiles_per_cond = cond_span // tm
    part = pl.BlockSpec((tm, 256), lambda i: (i, 0))
    row = pl.BlockSpec((1, D_MODEL), lambda i: (0, 0))
    return pl.pallas_call(
        _outproj_kernel,
        out_shape=jax.ShapeDtypeStruct((t, D_MODEL), F32),
        grid=(t // tm,),
        in_specs=[part, part, part, part,
                  pl.BlockSpec((tm, D_MODEL), lambda i: (i, 0)),
                  pl.BlockSpec((None, 1, ADA_CHUNKS * D_MODEL),
                               lambda i: (cond_base + i // tiles_per_cond, 0, 0)),
                  pl.BlockSpec((D_MODEL, D_MODEL), lambda i: (0, 0)), row, row],
        out_specs=pl.BlockSpec((tm, D_MODEL), lambda i: (i, 0)),
        compiler_params=_params("parallel"),
        name="outproj_postnorm",
    )(a_out, f_out, n_out, p_out, x, ada, w_out, ln_g, ln_b)


def _sorting_network(n):
    pairs = []
    p = 1
    while p < n:
        k = p
        while k >= 1:
            for j in range(k % p, n - k, 2 * k):
                for i in range(min(k, n - j - k)):
                    if (i + j) // (2 * p) == (i + j + k) // (2 * p):
                        pairs.append((i + j, i + j + k))
            k //= 2
        p *= 2
    return pairs


def _sort_descending(rows):
    rows = list(rows)
    for a, b in _sorting_network(len(rows)):
        rows[a], rows[b] = jnp.maximum(rows[a], rows[b]), jnp.minimum(rows[a], rows[b])
    return rows


def _sublane_max_all(x):
    x = jnp.maximum(x, pltpu.roll(x, 4, axis=0))
    x = jnp.maximum(x, pltpu.roll(x, 2, axis=0))
    return jnp.maximum(x, pltpu.roll(x, 1, axis=0))


def _extract_top(groups, count):
    groups = [list(g) for g in groups]
    out = []
    for it in range(count):
        head = groups[0][0]
        for g in groups[1:]:
            head = jnp.maximum(head, g[0])
        m = _sublane_max_all(head)
        out.append(m)
        remaining = count - it - 1
        if remaining == 0:
            break
        for g in groups:
            eq = g[0] == m
            keep = min(len(g), remaining)
            for i in range(keep):
                nxt = g[i + 1] if i + 1 < len(g) else jnp.full_like(m, -jnp.inf)
                g[i] = jnp.where(eq, nxt, g[i])
            del g[keep:]
    return out


def _peer_route(s1, s2):
    groups = PEER_N_KEYS // 8
    s1_rows = [s1[8 * i:8 * (i + 1)] for i in range(groups)]
    s2_rows = [s2[8 * i:8 * (i + 1)] for i in range(groups)]
    v1 = _extract_top([_sort_descending(s1_rows)], PEER_TOPK + 1)
    v2 = _extract_top([_sort_descending(s2_rows)], PEER_TOPK + 1)
    sub = lax.broadcasted_iota(jnp.int32, v1[0].shape, 0)
    lo = v1[0]
    hi = v1[8]
    for r in range(1, 8):
        lo = jnp.where(sub == r, v1[r], lo)
        hi = jnp.where(sub == r, v1[8 + r], hi)
    cand = [[lo + v2[b] for b in range(PEER_TOPK + 1)], [hi + v2[0]], [v1[PEER_TOPK] + v2[0]]]
    top = _extract_top(cand, PEER_TOPK + 1)
    den = jnp.ones_like(top[0])
    for tk in top[1:PEER_TOPK]:
        den = den + jnp.exp(tk - top[0])
    inv_den = 1.0 / den
    tau = 0.5 * (top[PEER_TOPK - 1] + top[PEER_TOPK])
    thr = [tau - r for r in s1_rows]
    coef = [jnp.exp(r - v1[0]) * inv_den for r in s1_rows]
    e2 = [jnp.exp(r - v2[0]) for r in s2_rows]
    return thr, coef, e2


def _peer_kernel(x_ref, ada_ref, wq_ref, k1_ref, k2_ref, u_ref, vt_ref, g_ref, b_ref, o_ref,
                 h_ref, ht_ref, yt_ref, act_ref, s2_ref, e2_ref, thr_ref, coef_ref):
    j = pl.program_id(1)
    tt = x_ref.shape[0]
    half = PEER_KEY_DIM // 2
    lanes = PEER_LANE_CHUNK

    @pl.when(j == 0)
    def _():
        sh = ada_ref[:, 3 * D_MODEL:4 * D_MODEL]
        sc = ada_ref[:, 4 * D_MODEL:5 * D_MODEL]
        h = _layer_norm(x_ref[...]) * (1.0 + sc) + sh
        h_ref[...] = h.astype(BF16)
        ht_ref[...] = h.T.astype(BF16)
        yt_ref[...] = jnp.zeros_like(yt_ref)

        def one_head(hd, carry):
            w_rows = wq_ref[pl.ds(pl.multiple_of(hd * PEER_KEY_DIM, PEER_KEY_DIM), PEER_KEY_DIM), :]
            q_t = lax.dot_general(w_rows, h_ref[...], (((1,), (1,)), ((), ())),
                                  preferred_element_type=F32)
            s1 = _dot(k1_ref[...], q_t[0:half])
            s2 = _dot(k2_ref[...], q_t[half:2 * half])
            s2_ref[hd] = s2
            for c in range(tt // lanes):
                cols = slice(c * lanes, (c + 1) * lanes)
                thr, coef, e2 = _peer_route(s1[:, cols], s2[:, cols])
                for i in range(PEER_N_KEYS // 8):
                    rows = pl.ds(8 * i, 8)
                    thr_ref[hd, rows, cols] = thr[i]
                    coef_ref[hd, rows, cols] = coef[i]
                    e2_ref[hd, rows, cols] = e2[i]
            return carry

        lax.fori_loop(0, PEER_HEADS, one_head, 0)

    keys_per_block = PEER_EXPERT_TILE // PEER_N_KEYS
    keys_per_sub = PEER_SUB_TILE // PEER_N_KEYS
    n_sub = PEER_EXPERT_TILE // PEER_SUB_TILE

    def up_proj(sub):
        return jnp.dot(u_ref[sub * PEER_SUB_TILE:(sub + 1) * PEER_SUB_TILE, :], ht_ref[...],
                       preferred_element_type=F32)

    acc = None
    a_next = up_proj(0)
    for sub in range(n_sub):
        sub_rows = slice(sub * PEER_SUB_TILE, (sub + 1) * PEER_SUB_TILE)
        a_sub = a_next
        if sub + 1 < n_sub:
            a_next = up_proj(sub + 1)
        for ii in range(keys_per_sub):
            i1 = j * keys_per_block + sub * keys_per_sub + ii
            rows = slice(ii * PEER_N_KEYS, (ii + 1) * PEER_N_KEYS)
            thr_rows = [thr_ref[hd, pl.ds(i1, 1), :] for hd in range(PEER_HEADS)]
            coef_rows = [coef_ref[hd, pl.ds(i1, 1), :] for hd in range(PEER_HEADS)]
            for c in range(tt // lanes):
                cols = slice(c * lanes, (c + 1) * lanes)
                w = None
                for hd in range(PEER_HEADS):
                    thr = thr_rows[hd][:, cols]
                    coef = coef_rows[hd][:, cols]
                    term = jnp.where(s2_ref[hd, :, cols] >= thr, e2_ref[hd, :, cols], 0.0) * coef
                    w = term if w is None else w + term
                a = a_sub[rows, cols]
                gelu = 0.5 * a * (1.0 + lax.erf(a * (2.0 ** -0.5)))
                act_ref[sub * PEER_SUB_TILE + ii * PEER_N_KEYS:
                        sub * PEER_SUB_TILE + (ii + 1) * PEER_N_KEYS, cols] = (w * gelu).astype(BF16)
        part = jnp.dot(vt_ref[:, sub_rows], act_ref[sub_rows, :], preferred_element_type=F32)
        acc = part if acc is None else acc + part
    yt_ref[...] += acc

    @pl.when(j == pl.num_programs(1) - 1)
    def _():
        gate = ada_ref[:, 5 * D_MODEL:6 * D_MODEL]
        z = ALPHA * x_ref[...] + gate * yt_ref[...].T
        o_ref[...] = _layer_norm(z) * g_ref[...] + b_ref[...]


def _peer_call(x, ada, cond_base, cond_span, layer, wq_t, k1, k2, u, v_t, ln_g, ln_b):
    t = x.shape[0]
    tt = PEER_TOKEN_TILE
    tiles_per_cond = cond_span // tt
    eb = PEER_EXPERT_TILE
    row = pl.BlockSpec((1, D_MODEL), lambda i, j: (0, 0))
    per_head = pltpu.VMEM((PEER_HEADS, PEER_N_KEYS, tt), F32)
    return pl.pallas_call(
        _peer_kernel,
        out_shape=jax.ShapeDtypeStruct((t, D_MODEL), F32),
        grid=(t // tt, PEER_N_EXPERTS // eb),
        in_specs=[
            pl.BlockSpec((tt, D_MODEL), lambda i, j: (i, 0)),
            pl.BlockSpec((None, 1, ADA_CHUNKS * D_MODEL),
                         lambda i, j: (cond_base + i // tiles_per_cond, 0, 0)),
            pl.BlockSpec((PEER_HEADS * PEER_KEY_DIM, D_MODEL), lambda i, j: (0, 0)),
            pl.BlockSpec((PEER_N_KEYS, PEER_KEY_DIM // 2), lambda i, j: (0, 0)),
            pl.BlockSpec((PEER_N_KEYS, PEER_KEY_DIM // 2), lambda i, j: (0, 0)),
            pl.BlockSpec((None, eb, D_MODEL), lambda i, j: (layer, j, 0)),
            pl.BlockSpec((None, D_MODEL, eb), lambda i, j: (layer, 0, j)),
            row, row,
        ],
        out_specs=pl.BlockSpec((tt, D_MODEL), lambda i, j: (i, 0)),
        scratch_shapes=[
            pltpu.VMEM((tt, D_MODEL), BF16),
            pltpu.VMEM((D_MODEL, tt), BF16),
            pltpu.VMEM((D_MODEL, tt), F32),
            pltpu.VMEM((eb, tt), BF16),
            per_head, per_head, per_head, per_head,
        ],
        compiler_params=_params("parallel", "arbitrary"),
        name="peer",
    )(x, ada, wq_t, k1, k2, u, v_t, ln_g, ln_b)


def _rope_tables(seq_len):
    t = np.arange(seq_len)
    row = (t // GRID_W).astype(np.float64)
    col = (t % GRID_W).astype(np.float64)
    nf = HEAD_DIM // 4
    inv = ROPE_THETA ** (-np.arange(nf, dtype=np.float64) / nf)
    d = np.arange(HEAD_DIM)
    pos = np.where(d[None, :] < HEAD_DIM // 2, row[:, None], col[:, None])
    ang = pos * inv[d % nf][None, :]
    sign = np.where((d % (HEAD_DIM // 2)) < nf, -1.0, 1.0)
    cos = np.tile(np.cos(ang), (1, ATTN_HEADS))
    sin = np.tile(np.sin(ang) * sign[None, :], (1, ATTN_HEADS))
    return jnp.asarray(cos, F32), jnp.asarray(sin, F32)


def _head_mean_matrix():
    m = np.kron(np.eye(ATTN_HEADS), np.full((HEAD_DIM, HEAD_DIM), 1.0 / HEAD_DIM))
    return jnp.asarray(m, BF16)


def _block_diag(w):
    g, a, b = w.shape
    out = jnp.zeros((g * a, g * b), w.dtype)
    for i in range(g):
        out = out.at[i * a:(i + 1) * a, i * b:(i + 1) * b].set(w[i])
    return out


def _trunk_layer(x, ada, cond_base, batch, seq_len, lw, ctx_cache):
    rope = ctx_cache is not None
    if rope:
        cos_t, sin_t = _rope_tables(seq_len)
    else:
        cos_t = jnp.ones((seq_len, 256), F32)
        sin_t = jnp.zeros((seq_len, 256), F32)
    cond_span = seq_len if rope else batch * seq_len
    aq, ak, av, fx, nq, nk, nv, px = _inproj_call(
        x, ada, cond_base, cond_span, seq_len, lw["w_in"], lw["gq"], lw["gk"], lw["head_mean"],
        cos_t, sin_t, rope)
    if ctx_cache is None:
        a_out, n_out = _ctx_attn_call(aq, ak, av, nq, nk, nv, batch, seq_len)
    else:
        ck_a, cv_a, ck_n, cv_n = ctx_cache
        a_out = _lat_attn_call(aq, ak, av, ck_a, cv_a, batch, seq_len)
        n_out = _na_call(nq, nk, nv, ck_n, cv_n, lw["na_bias"], batch, seq_len)
    f_out = _fourier_call(fx, lw["w_fourier"], batch, seq_len)
    p_out = _pool_call(px, lw["w_pool"], lw["pool_scale"], batch, seq_len)
    x = _outproj_call(a_out, f_out, n_out, p_out, x, ada, cond_base, cond_span,
                      lw["w_out"], lw["ln1_g"], lw["ln1_b"])
    x = _peer_call(x, ada, cond_base, cond_span, lw["layer"], lw["wq_t"], lw["k1"], lw["k2"],
                   lw["u"], lw["v_t"], lw["ln2_g"], lw["ln2_b"])
    return x, (ak, av, nk, nv)


def kernel(x_prompt, x_sample, cache_attn_k, cache_attn_v, cache_na_k, cache_na_v, c, c_ctx,
           w_ada, b_ada, w_in, q_norm_g, k_norm_g, w_fourier, na_bias, w_pool, pool_scale,
           w_out, ln1_g, ln1_b, ln2_g, ln2_b, peer_wq, peer_k1, peer_k2, peer_u, peer_v):
    batch, seq, d = x_prompt.shape
    dec_batch, dec_seq, _ = x_sample.shape
    xp = x_prompt.reshape(batch * seq, d)
    xs = x_sample.reshape(dec_batch * dec_seq, d)

    cond = jnp.zeros((COND_ROWS, d), F32).at[0].set(c_ctx).at[1:1 + dec_batch].set(c)
    ada_all = _ada_call(cond, w_ada, b_ada)
    head_mean = _head_mean_matrix()
    rows = dec_seq // GRID_W

    def heads_first(cache):
        return cache.transpose(1, 0, 3, 2, 4)

    cak, cav = heads_first(cache_attn_k), heads_first(cache_attn_v)
    cnk, cnv = heads_first(cache_na_k), heads_first(cache_na_v)

    u_all = peer_u.astype(BF16)
    v_t_all = jnp.swapaxes(peer_v.astype(BF16), 1, 2)

    new_kv = [[], [], [], []]
    for l in range(DEPTH):
        lw = {
            "layer": l,
            "w_in": w_in[l].astype(BF16),
            "gq": jnp.tile(q_norm_g[l], ATTN_HEADS).reshape(1, -1),
            "gk": jnp.tile(k_norm_g[l], ATTN_KV_HEADS).reshape(1, -1),
            "head_mean": head_mean,
            "w_fourier": w_fourier[l].astype(BF16),
            "na_bias": _na_bias_patterns(na_bias[l], rows),
            "w_pool": _block_diag(w_pool[l]).astype(BF16),
            "pool_scale": pool_scale[l].reshape(1, -1),
            "w_out": w_out[l].astype(BF16),
            "ln1_g": ln1_g[l].reshape(1, -1), "ln1_b": ln1_b[l].reshape(1, -1),
            "ln2_g": ln2_g[l].reshape(1, -1), "ln2_b": ln2_b[l].reshape(1, -1),
            "wq_t": peer_wq[l].T.astype(BF16),
            "k1": peer_k1[l].astype(BF16), "k2": peer_k2[l].astype(BF16),
            "u": u_all,
            "v_t": v_t_all,
        }
        ada = ada_all[l].reshape(COND_ROWS, 1, ADA_CHUNKS * d)
        xp, kv = _trunk_layer(xp, ada, 0, batch, seq, lw, None)
        for dst, arr in zip(new_kv, kv):
            heads = arr.shape[0]
            dst.append(arr.reshape(heads, batch, seq, HEAD_DIM).transpose(1, 2, 0, 3))
        xs, _ = _trunk_layer(xs, ada, 1, dec_batch, dec_seq, lw,
                             (cak[l], cav[l], cnk[l], cnv[l]))
    outs = [jnp.stack(lst, axis=1) for lst in new_kv]
    return (xp.reshape(batch, seq, d), xs.reshape(dec_batch, dec_seq, d), *outs)
```

```python
import functools
import math

import numpy as np
import jax
import jax.numpy as jnp
from jax import lax
from jax.experimental import pallas as pl
from jax.experimental.pallas import tpu as pltpu

D_MODEL = 1024
DEPTH = 2
GRID_W = 64
HEAD_DIM = 64
GROUP_WIDTH = 256
ATTN_HEADS = 4
ATTN_KV_HEADS = 2
NA_HEADS = 4
NA_KH = 8
NA_KW = 16
FOURIER_HEADS = 4
POOL_WINDOWS = (2, 4, 8, 16)
POOL_GROUP_W = GROUP_WIDTH // len(POOL_WINDOWS)
ROPE_THETA = 10000.0
ATTN_SCALE = HEAD_DIM ** -0.5
PEER_HEADS = 8
PEER_N_KEYS = 128
PEER_N_EXPERTS = PEER_N_KEYS * PEER_N_KEYS
PEER_KEY_DIM = 256
PEER_TOPK = 16
ALPHA = (2 * DEPTH) ** 0.25
LN_EPS = 1e-6
RMS_EPS = 1e-6
ADA_CHUNKS = 6
P_IN = 1792
COND_ROWS = 8

TOKEN_TILE = 256
ATTN_Q_TILE = 512
DFT_ROW_TILE = 512
NA_GROUP_ROWS = 4
PEER_TOKEN_TILE = 512
PEER_EXPERT_TILE = 2048
PEER_SUB_TILE = 256
PEER_LANE_CHUNK = 128
ADA_COL_TILE = 1024
NEG_BIG = -1e30
VMEM_LIMIT = 56 * 1024 * 1024

F32 = jnp.float32
BF16 = jnp.bfloat16


def _params(*sem):
    return pltpu.CompilerParams(dimension_semantics=sem, vmem_limit_bytes=VMEM_LIMIT)


def _dot(a, b):
    return jnp.dot(a.astype(BF16), b.astype(BF16), preferred_element_type=F32)


def _dot_nt(a, b):
    return lax.dot_general(a.astype(BF16), b.astype(BF16), (((1,), (1,)), ((), ())),
                           preferred_element_type=F32)


def _layer_norm(x):
    mu = jnp.mean(x, axis=-1, keepdims=True)
    xc = x - mu
    var = jnp.mean(xc * xc, axis=-1, keepdims=True)
    return xc * lax.rsqrt(var + LN_EPS)


def _ada_kernel(cond_ref, w_ref, b_ref, o_ref):
    c = cond_ref[...]
    s = c / (1.0 + jnp.exp(-c))
    o_ref[...] = jnp.dot(s, w_ref[...], precision=lax.Precision.HIGHEST,
                         preferred_element_type=F32) + b_ref[...]


def _ada_call(cond, w_ada, b_ada):
    n = ADA_CHUNKS * D_MODEL
    return pl.pallas_call(
        _ada_kernel,
        out_shape=jax.ShapeDtypeStruct((DEPTH, COND_ROWS, n), F32),
        grid=(DEPTH, n // ADA_COL_TILE),
        in_specs=[
            pl.BlockSpec((COND_ROWS, D_MODEL), lambda l, j: (0, 0)),
            pl.BlockSpec((None, D_MODEL, ADA_COL_TILE), lambda l, j: (l, 0, j)),
            pl.BlockSpec((None, 1, ADA_COL_TILE), lambda l, j: (l, 0, j)),
        ],
        out_specs=pl.BlockSpec((None, COND_ROWS, ADA_COL_TILE), lambda l, j: (l, 0, j)),
        compiler_params=_params("parallel", "arbitrary"),
        name="ada_proj",
    )(cond, w_ada, b_ada.reshape(DEPTH, 1, n))


def _head_mean_sq(x, g):
    sq = x * x
    hi = sq.astype(BF16)
    lo = (sq - hi.astype(F32)).astype(BF16)
    return (jnp.dot(hi, g, preferred_element_type=F32)
            + jnp.dot(lo, g, preferred_element_type=F32))


def _rotary(x, cos, sin_signed):
    w = x.shape[-1]
    fwd = pltpu.roll(x, w - 16, axis=1)
    bwd = pltpu.roll(x, 16, axis=1)
    lane = lax.broadcasted_iota(jnp.int32, x.shape, 1)
    partner = jnp.where((lane % 32) < 16, fwd, bwd)
    return x * cos + partner * sin_signed


def _inproj_kernel(x_ref, ada_ref, w_ref, gq_ref, gk_ref, hm_ref, cos_ref, sin_ref,
                   aq_ref, ak_ref, av_ref, fx_ref, nq_ref, nk_ref, nv_ref, px_ref, *, rope):
    x = x_ref[...]
    sh = ada_ref[:, 0:D_MODEL]
    sc = ada_ref[:, D_MODEL:2 * D_MODEL]
    h = _layer_norm(x) * (1.0 + sc) + sh
    proj = _dot(h, w_ref[...])
    aq = proj[:, 0:256]
    ak = proj[:, 256:384]
    hm = hm_ref[...]
    aq = aq * lax.rsqrt(_head_mean_sq(aq, hm) + RMS_EPS) * gq_ref[...]
    ak = ak * lax.rsqrt(_head_mean_sq(ak, hm[0:128, 0:128]) + RMS_EPS) * gk_ref[...]
    if rope:
        cos = cos_ref[...]
        sin = sin_ref[...]
        aq = _rotary(aq, cos, sin)
        ak = _rotary(ak, cos[:, 0:128], sin[:, 0:128])
    for hd in range(ATTN_HEADS):
        aq_ref[hd] = aq[:, hd * 64:(hd + 1) * 64]
    for hd in range(ATTN_KV_HEADS):
        ak_ref[hd] = ak[:, hd * 64:(hd + 1) * 64]
        av_ref[hd] = proj[:, 384 + hd * 64:384 + (hd + 1) * 64]
    fx_ref[...] = proj[:, 512:768]
    for hd in range(NA_HEADS):
        nq_ref[hd] = proj[:, 768 + hd * 64:768 + (hd + 1) * 64]
        nk_ref[hd] = proj[:, 1024 + hd * 64:1024 + (hd + 1) * 64]
        nv_ref[hd] = proj[:, 1280 + hd * 64:1280 + (hd + 1) * 64]
    px_ref[...] = proj[:, 1536:1792]


def _inproj_call(x, ada, cond_base, cond_span, seq_len, w_in, gq, gk, head_mean, cos_t, sin_t,
                 rope):
    t = x.shape[0]
    tm = TOKEN_TILE
    tiles_per_seq = seq_len // tm
    tiles_per_cond = cond_span // tm
    hm = lambda n: jax.ShapeDtypeStruct((n, t, HEAD_DIM), F32)
    hspec = lambda n: pl.BlockSpec((n, tm, HEAD_DIM), lambda i: (0, i, 0))
    const = lambda shape: pl.BlockSpec(shape, lambda i: tuple(0 for _ in shape))
    return pl.pallas_call(
        functools.partial(_inproj_kernel, rope=rope),
        out_shape=(hm(4), hm(2), hm(2), jax.ShapeDtypeStruct((t, 256), F32),
                   hm(4), hm(4), hm(4), jax.ShapeDtypeStruct((t, 256), F32)),
        grid=(t // tm,),
        in_specs=[
            pl.BlockSpec((tm, D_MODEL), lambda i: (i, 0)),
            pl.BlockSpec((None, 1, ADA_CHUNKS * D_MODEL),
                         lambda i: (cond_base + i // tiles_per_cond, 0, 0)),
            const((D_MODEL, P_IN)),
            const((1, 256)), const((1, 128)), const((256, 256)),
            pl.BlockSpec((tm, 256), lambda i: (i % tiles_per_seq, 0)),
            pl.BlockSpec((tm, 256), lambda i: (i % tiles_per_seq, 0)),
        ],
        out_specs=(hspec(4), hspec(2), hspec(2), pl.BlockSpec((tm, 256), lambda i: (i, 0)),
                   hspec(4), hspec(4), hspec(4), pl.BlockSpec((tm, 256), lambda i: (i, 0))),
        compiler_params=_params("parallel"),
        name="inproj_rope" if rope else "inproj",
    )(x, ada, w_in, gq, gk, head_mean, cos_t, sin_t)


def _softmax_attend(q, k, v):
    s = _dot_nt(q, k) * ATTN_SCALE
    m = jnp.max(s, axis=-1, keepdims=True)
    p = jnp.exp(s - m)
    return _dot(p, v) / jnp.sum(p, axis=-1, keepdims=True)


def _ctx_attn_kernel(aq_ref, ak_ref, av_ref, nq_ref, nk_ref, nv_ref, a_ref, n_ref):
    outs = [_softmax_attend(aq_ref[hd], ak_ref[hd // 2], av_ref[hd // 2])
            for hd in range(ATTN_HEADS)]
    a_ref[...] = jnp.concatenate(outs, axis=-1)
    outs = [_softmax_attend(nq_ref[hd], nk_ref[hd], nv_ref[hd]) for hd in range(NA_HEADS)]
    n_ref[...] = jnp.concatenate(outs, axis=-1)


def _ctx_attn_call(aq, ak, av, nq, nk, nv, batch, seq_len):
    t = batch * seq_len
    hspec = lambda n: pl.BlockSpec((n, seq_len, HEAD_DIM), lambda b: (0, b, 0))
    ospec = pl.BlockSpec((seq_len, 256), lambda b: (b, 0))
    return pl.pallas_call(
        _ctx_attn_kernel,
        out_shape=(jax.ShapeDtypeStruct((t, 256), F32), jax.ShapeDtypeStruct((t, 256), F32)),
        grid=(batch,),
        in_specs=[hspec(4), hspec(2), hspec(2), hspec(4), hspec(4), hspec(4)],
        out_specs=(ospec, ospec),
        compiler_params=_params("parallel"),
        name="ctx_attention",
    )(aq, ak, av, nq, nk, nv)


def _lat_attn_kernel(q_ref, k_ref, v_ref, kc_ref, vc_ref, o_ref):
    k = k_ref[...]
    v = v_ref[...]
    kc = kc_ref[...]
    vc = vc_ref[...]
    outs = []
    for g in range(2):
        q = q_ref[g]
        s_l = _dot_nt(q, k) * ATTN_SCALE
        s_c = _dot_nt(q, kc) * ATTN_SCALE
        m = jnp.maximum(jnp.max(s_l, axis=-1, keepdims=True), jnp.max(s_c, axis=-1, keepdims=True))
        p_l = jnp.exp(s_l - m)
        p_c = jnp.exp(s_c - m)
        den = jnp.sum(p_l, axis=-1, keepdims=True) + jnp.sum(p_c, axis=-1, keepdims=True)
        outs.append((_dot(p_l, v) + _dot(p_c, vc)) / den)
    o_ref[...] = jnp.concatenate(outs, axis=-1)


def _lat_attn_call(aq, ak, av, kc, vc, batch, seq_len):
    t = batch * seq_len
    tq = ATTN_Q_TILE
    nqb = seq_len // tq
    return pl.pallas_call(
        _lat_attn_kernel,
        out_shape=jax.ShapeDtypeStruct((t, 256), F32),
        grid=(batch, ATTN_KV_HEADS, nqb),
        in_specs=[
            pl.BlockSpec((2, tq, HEAD_DIM), lambda b, g, i: (g, b * nqb + i, 0)),
            pl.BlockSpec((None, seq_len, HEAD_DIM), lambda b, g, i: (g, b, 0)),
            pl.BlockSpec((None, seq_len, HEAD_DIM), lambda b, g, i: (g, b, 0)),
            pl.BlockSpec((None, None, kc.shape[2], HEAD_DIM), lambda b, g, i: (b, g, 0, 0)),
            pl.BlockSpec((None, None, kc.shape[2], HEAD_DIM), lambda b, g, i: (b, g, 0, 0)),
        ],
        out_specs=pl.BlockSpec((tq, 128), lambda b, g, i: (b * nqb + i, g)),
        compiler_params=_params("parallel", "parallel", "arbitrary"),
        name="lat_attention",
    )(aq, ak, av, kc, vc)


def _na_geometry(rows):
    kh = min(NA_KH, rows)
    gr = NA_GROUP_ROWS
    kb = min(gr + kh, rows)
    starts, classes, tables = [], [], []
    for g in range(rows // gr):
        us = int(np.clip(gr * g - kh // 2, 0, rows - kb))
        rel = np.full((gr, kb), -1, np.int64)
        for dr in range(gr):
            r = gr * g + dr
            rs = int(np.clip(r - kh // 2, 0, rows - kh))
            assert us <= rs and rs + kh <= us + kb
            for k in range(kh):
                rel[dr, rs + k - us] = rs + k - r + (NA_KH - 1)
        for ci, tab in enumerate(tables):
            if np.array_equal(tab, rel):
                classes.append(ci)
                break
        else:
            classes.append(len(tables))
            tables.append(rel)
        starts.append(us)
    return kb, starts, classes, tables


def _na_kernel(start_ref, class_ref, q_ref, k_ref, v_ref, kc_ref, vc_ref, bias_ref, o_ref, *,
               rows, key_rows):
    gq = NA_GROUP_ROWS * GRID_W
    gk = key_rows * GRID_W

    def one_group(g, carry):
        q_off = pl.multiple_of(g * gq, gq)
        k_off = pl.multiple_of(start_ref[g] * GRID_W, GRID_W)
        cls = class_ref[g]
        outs = []
        for hd in range(2):
            q = q_ref[hd, pl.ds(q_off, gq), :]
            kw = k_ref[hd, pl.ds(k_off, gk), :]
            vw = v_ref[hd, pl.ds(k_off, gk), :]
            s_w = _dot_nt(q, kw) * ATTN_SCALE + bias_ref[hd, cls]
            s_c = _dot_nt(q, kc_ref[hd]) * ATTN_SCALE
            m = jnp.maximum(jnp.max(s_w, axis=-1, keepdims=True),
                            jnp.max(s_c, axis=-1, keepdims=True))
            p_w = jnp.exp(s_w - m)
            p_c = jnp.exp(s_c - m)
            den = jnp.sum(p_w, axis=-1, keepdims=True) + jnp.sum(p_c, axis=-1, keepdims=True)
            outs.append((_dot(p_w, vw) + _dot(p_c, vc_ref[hd])) / den)
        o_ref[pl.ds(q_off, gq), :] = jnp.concatenate(outs, axis=-1)
        return carry

    lax.fori_loop(0, rows // NA_GROUP_ROWS, one_group, 0)


def _na_call(nq, nk, nv, kc, vc, bias, batch, seq_len):
    t = batch * seq_len
    rows = seq_len // GRID_W
    key_rows, starts, classes, _ = _na_geometry(rows)
    lspec = pl.BlockSpec((2, seq_len, HEAD_DIM), lambda b, hp, *_: (hp, b, 0))
    cspec = pl.BlockSpec((None, 2, kc.shape[2], HEAD_DIM), lambda b, hp, *_: (b, hp, 0, 0))
    return pl.pallas_call(
        functools.partial(_na_kernel, rows=rows, key_rows=key_rows),
        out_shape=jax.ShapeDtypeStruct((t, 256), F32),
        grid_spec=pltpu.PrefetchScalarGridSpec(
            num_scalar_prefetch=2,
            grid=(batch, NA_HEADS // 2),
            in_specs=[lspec, lspec, lspec, cspec, cspec,
                      pl.BlockSpec((2,) + bias.shape[1:], lambda b, hp, *_: (hp, 0, 0, 0))],
            out_specs=pl.BlockSpec((seq_len, 128), lambda b, hp, *_: (b, hp)),
        ),
        compiler_params=_params("parallel", "parallel"),
        name="neighbourhood_attention",
    )(jnp.asarray(starts, jnp.int32), jnp.asarray(classes, jnp.int32), nq, nk, nv, kc, vc, bias)


def _na_bias_patterns(bias_table, rows):
    key_rows, _, _, tables = _na_geometry(rows)
    rel = np.stack(tables)
    c = np.arange(GRID_W)
    cs = np.clip(c - NA_KW // 2, 0, GRID_W - NA_KW)
    kc = np.arange(GRID_W)
    inside = (kc[None, :] >= cs[:, None]) & (kc[None, :] < cs[:, None] + NA_KW)
    col_rel = np.clip(kc[None, :] - c[:, None] + (NA_KW - 1), 0, 2 * NA_KW - 2)
    row_pick = (rel[..., None] == np.arange(2 * NA_KH - 1)).astype(np.float32)
    col_pick = (col_rel[:, :, None] == np.arange(2 * NA_KW - 1)).astype(np.float32)
    b = jnp.einsum("hrs,cqs->hrcq", bias_table, col_pick, precision=lax.Precision.HIGHEST)
    b = jnp.einsum("gdkr,hrcq->hgdckq", row_pick, b, precision=lax.Precision.HIGHEST)
    keep = (rel >= 0)[:, :, None, :, None] & inside[None, None, :, None, :]
    b = jnp.where(keep[None], b, NEG_BIG)
    return b.reshape(bias_table.shape[0], rel.shape[0], NA_GROUP_ROWS * GRID_W,
                     key_rows * GRID_W)


def _fourier_kernel(x_ref, cl_ref, sl_ref, cc_ref, sc_ref, w_ref, o_ref, *, norm):
    x = x_ref[...]
    xc = _dot(x, cc_ref[...])
    xs = _dot(x, sc_ref[...])
    y = (_dot(cl_ref[...], xc) - _dot(sl_ref[...], xs)) * norm
    o_ref[...] = _dot(y, w_ref[...])


def _fourier_call(fx, w_fourier, batch, seq_len):
    t = batch * seq_len
    tm = min(DFT_ROW_TILE, seq_len)
    nmb = seq_len // tm
    j = np.arange(seq_len, dtype=np.float64)
    ang_l = 2.0 * np.pi * ((j[:, None] * j[None, :]) % seq_len) / seq_len
    hw = GROUP_WIDTH // FOURIER_HEADS
    c = np.arange(hw, dtype=np.float64)
    ang_c = 2.0 * np.pi * ((c[:, None] * c[None, :]) % hw) / hw
    eye = np.eye(FOURIER_HEADS)
    cl = jnp.asarray(np.cos(ang_l), F32)
    sl = jnp.asarray(np.sin(ang_l), F32)
    cc = jnp.asarray(np.kron(eye, np.cos(ang_c)), F32)
    sc = jnp.asarray(np.kron(eye, np.sin(ang_c)), F32)
    const = lambda shape: pl.BlockSpec(shape, lambda i, b: (0, 0))
    return pl.pallas_call(
        functools.partial(_fourier_kernel, norm=float((seq_len * hw) ** -0.5)),
        out_shape=jax.ShapeDtypeStruct((t, 256), F32),
        grid=(nmb, batch),
        in_specs=[
            pl.BlockSpec((seq_len, 256), lambda i, b: (b, 0)),
            pl.BlockSpec((tm, seq_len), lambda i, b: (i, 0)),
            pl.BlockSpec((tm, seq_len), lambda i, b: (i, 0)),
            const((256, 256)), const((256, 256)), const((256, 256)),
        ],
        out_specs=pl.BlockSpec((tm, 256), lambda i, b: (b * nmb + i, 0)),
        compiler_params=_params("parallel", "arbitrary"),
        name="fourier_mix",
    )(fx, cl, sl, cc, sc, w_fourier)


POOL_PAD = 8


def _pool_kernel(x_ref, w_ref, s_ref, o_ref, pad_ref, *, seq_len):
    x = x_ref[...]
    zeros = jnp.zeros((POOL_PAD, GROUP_WIDTH), F32)
    pad_ref[0:POOL_PAD, :] = zeros
    pad_ref[POOL_PAD + seq_len:2 * POOL_PAD + seq_len, :] = zeros
    pad_ref[POOL_PAD:POOL_PAD + seq_len, :] = x

    def shifted(off):
        return pad_ref[POOL_PAD + off:POOL_PAD + off + seq_len, :]

    t = lax.broadcasted_iota(jnp.int32, (seq_len, GROUP_WIDTH), 0)
    lane_group = lax.broadcasted_iota(jnp.int32, (seq_len, GROUP_WIDTH), 1) // POOL_GROUP_W
    acc = x
    done = 0
    pooled = jnp.zeros_like(x)
    for g, w in enumerate(POOL_WINDOWS):
        half = w // 2
        for off in list(range(-half, -done)) + list(range(max(done, 1), half)):
            acc = acc + shifted(off)
        done = half
        cnt = jnp.minimum(t + half, seq_len) - jnp.maximum(t - half, 0)
        pooled = jnp.where(lane_group == g, acc / cnt.astype(F32), pooled)
    o_ref[...] = _dot(pooled - x, w_ref[...]) * s_ref[...]


def _pool_call(px, w_pool_bd, pool_scale, batch, seq_len):
    t = batch * seq_len
    return pl.pallas_call(
        functools.partial(_pool_kernel, seq_len=seq_len),
        out_shape=jax.ShapeDtypeStruct((t, 256), F32),
        grid=(batch,),
        in_specs=[
            pl.BlockSpec((seq_len, 256), lambda b: (b, 0)),
            pl.BlockSpec((256, 256), lambda b: (0, 0)),
            pl.BlockSpec((1, 256), lambda b: (0, 0)),
        ],
        out_specs=pl.BlockSpec((seq_len, 256), lambda b: (b, 0)),
        scratch_shapes=[pltpu.VMEM((seq_len + 2 * POOL_PAD, GROUP_WIDTH), F32)],
        compiler_params=_params("parallel"),
        name="multiscale_pool",
    )(px, w_pool_bd, pool_scale)


def _outproj_kernel(a_ref, f_ref, n_ref, p_ref, x_ref, ada_ref, w_ref, g_ref, b_ref, o_ref):
    mix = (_dot(a_ref[...], w_ref[0:256, :]) + _dot(f_ref[...], w_ref[256:512, :])
           + _dot(n_ref[...], w_ref[512:768, :]) + _dot(p_ref[...], w_ref[768:1024, :]))
    gate = ada_ref[:, 2 * D_MODEL:3 * D_MODEL]
    z = ALPHA * x_ref[...] + gate * mix
    o_ref[...] = _layer_norm(z) * g_ref[...] + b_ref[...]


def _outproj_call(a_out, f_out, n_out, p_out, x, ada, cond_base, cond_span, w_out, ln_g, ln_b):
    t = x.shape[0]
    tm = TOKEN_TILE
    tiles_per_cond = cond_span // tm
    part = pl.BlockSpec((tm, 256), lambda i: (i, 0))
    row = pl.BlockSpec((1, D_MODEL), lambda i: (0, 0))
    return pl.pallas_call(
        _outproj_kernel,
        out_shape=jax.ShapeDtypeStruct((t, D_MODEL), F32),
        grid=(t // tm,),
        in_specs=[part, part, part, part,
                  pl.BlockSpec((tm, D_MODEL), lambda i: (i, 0)),
                  pl.BlockSpec((None, 1, ADA_CHUNKS * D_MODEL),
                               lambda i: (cond_base + i // tiles_per_cond, 0, 0)),
                  pl.BlockSpec((D_MODEL, D_MODEL), lambda i: (0, 0)), row, row],
        out_specs=pl.BlockSpec((tm, D_MODEL), lambda i: (i, 0)),
        compiler_params=_params("parallel"),
        name="outproj_postnorm",
    )(a_out, f_out, n_out, p_out, x, ada, w_out, ln_g, ln_b)


def _sorting_network(n):
    pairs = []
    p = 1
    while p < n:
        k = p
        while k >= 1:
            for j in range(k % p, n - k, 2 * k):
                for i in range(min(k, n - j - k)):
                    if (i + j) // (2 * p) == (i + j + k) // (2 * p):
                        pairs.append((i + j, i + j + k))
            k //= 2
        p *= 2
    return pairs


def _sort_descending(rows):
    rows = list(rows)
    for a, b in _sorting_network(len(rows)):
        rows[a], rows[b] = jnp.maximum(rows[a], rows[b]), jnp.minimum(rows[a], rows[b])
    return rows


def _sublane_max_all(x):
    x = jnp.maximum(x, pltpu.roll(x, 4, axis=0))
    x = jnp.maximum(x, pltpu.roll(x, 2, axis=0))
    return jnp.maximum(x, pltpu.roll(x, 1, axis=0))


def _extract_top(groups, count):
    groups = [list(g) for g in groups]
    out = []
    for it in range(count):
        head = groups[0][0]
        for g in groups[1:]:
            head = jnp.maximum(head, g[0])
        m = _sublane_max_all(head)
        out.append(m)
        remaining = count - it - 1
        if remaining == 0:
            break
        for g in groups:
            eq = g[0] == m
            keep = min(len(g), remaining)
            for i in range(keep):
                nxt = g[i + 1] if i + 1 < len(g) else jnp.full_like(m, -jnp.inf)
                g[i] = jnp.where(eq, nxt, g[i])
            del g[keep:]
    return out


def _peer_route(s1, s2):
    groups = PEER_N_KEYS // 8
    s1_rows = [s1[8 * i:8 * (i + 1)] for i in range(groups)]
    s2_rows = [s2[8 * i:8 * (i + 1)] for i in range(groups)]
    v1 = _extract_top([_sort_descending(s1_rows)], PEER_TOPK + 1)
    v2 = _extract_top([_sort_descending(s2_rows)], PEER_TOPK + 1)
    sub = lax.broadcasted_iota(jnp.int32, v1[0].shape, 0)
    lo = v1[0]
    hi = v1[8]
    for r in range(1, 8):
        lo = jnp.where(sub == r, v1[r], lo)
        hi = jnp.where(sub == r, v1[8 + r], hi)
    cand = [[lo + v2[b] for b in range(PEER_TOPK + 1)], [hi + v2[0]], [v1[PEER_TOPK] + v2[0]]]
    top = _extract_top(cand, PEER_TOPK + 1)
    den = jnp.ones_like(top[0])
    for tk in top[1:PEER_TOPK]:
        den = den + jnp.exp(tk - top[0])
    half_inv_den = 0.5 / den
    tau = 0.5 * (top[PEER_TOPK - 1] + top[PEER_TOPK])
    thr = [tau - r for r in s1_rows]
    coef = [jnp.exp(r - v1[0]) * half_inv_den for r in s1_rows]
    e2 = [jnp.exp(r - v2[0]) for r in s2_rows]
    return thr, coef, e2


def _peer_kernel(x_ref, ada_ref, wq_ref, k1_ref, k2_ref, u_ref, vt_ref, g_ref, b_ref, o_ref,
                 h_ref, ht_ref, yt_ref, act_ref, s2_ref, e2_ref, thr_ref, coef_ref):
    j = pl.program_id(1)
    tt = x_ref.shape[0]
    half = PEER_KEY_DIM // 2
    lanes = PEER_LANE_CHUNK

    @pl.when(j == 0)
    def _():
        sh = ada_ref[:, 3 * D_MODEL:4 * D_MODEL]
        sc = ada_ref[:, 4 * D_MODEL:5 * D_MODEL]
        h = _layer_norm(x_ref[...]) * (1.0 + sc) + sh
        h_ref[...] = h.astype(BF16)
        ht_ref[...] = h.T.astype(BF16)
        yt_ref[...] = jnp.zeros_like(yt_ref)

        def one_head(hd, carry):
            w_rows = wq_ref[pl.ds(pl.multiple_of(hd * PEER_KEY_DIM, PEER_KEY_DIM), PEER_KEY_DIM), :]
            q_t = lax.dot_general(w_rows, h_ref[...], (((1,), (1,)), ((), ())),
                                  preferred_element_type=F32)
            s1 = _dot(k1_ref[...], q_t[0:half])
            s2 = _dot(k2_ref[...], q_t[half:2 * half])
            s2_ref[hd] = s2
            for c in range(tt // lanes):
                cols = slice(c * lanes, (c + 1) * lanes)
                thr, coef, e2 = _peer_route(s1[:, cols], s2[:, cols])
                for i in range(PEER_N_KEYS // 8):
                    rows = pl.ds(8 * i, 8)
                    thr_ref[hd, rows, cols] = thr[i]
                    coef_ref[hd, rows, cols] = coef[i]
                    e2_ref[hd, rows, cols] = e2[i]
            return carry

        lax.fori_loop(0, PEER_HEADS, one_head, 0)

    keys_per_block = PEER_EXPERT_TILE // PEER_N_KEYS
    keys_per_sub = PEER_SUB_TILE // PEER_N_KEYS
    n_sub = PEER_EXPERT_TILE // PEER_SUB_TILE

    def up_proj(sub):
        return jnp.dot(u_ref[sub * PEER_SUB_TILE:(sub + 1) * PEER_SUB_TILE, :], ht_ref[...],
                       preferred_element_type=F32)

    acc = None
    a_next = up_proj(0)
    for sub in range(n_sub):
        sub_rows = slice(sub * PEER_SUB_TILE, (sub + 1) * PEER_SUB_TILE)
        a_sub = a_next
        if sub + 1 < n_sub:
            a_next = up_proj(sub + 1)
        for ii in range(keys_per_sub):
            i1 = j * keys_per_block + sub * keys_per_sub + ii
            rows = slice(ii * PEER_N_KEYS, (ii + 1) * PEER_N_KEYS)
            thr_rows = [thr_ref[hd, pl.ds(i1, 1), :] for hd in range(PEER_HEADS)]
            coef_rows = [coef_ref[hd, pl.ds(i1, 1), :] for hd in range(PEER_HEADS)]
            for c in range(tt // lanes):
                cols = slice(c * lanes, (c + 1) * lanes)
                w = None
                for hd in range(PEER_HEADS):
                    thr = thr_rows[hd][:, cols]
                    coef = coef_rows[hd][:, cols]
                    term = jnp.where(s2_ref[hd, :, cols] >= thr, e2_ref[hd, :, cols], 0.0) * coef
                    w = term if w is None else w + term
                a = a_sub[rows, cols]
                gelu2 = a * (1.0 + lax.erf(a * (2.0 ** -0.5)))
                act_ref[sub * PEER_SUB_TILE + ii * PEER_N_KEYS:
                        sub * PEER_SUB_TILE + (ii + 1) * PEER_N_KEYS, cols] = (w * gelu2).astype(BF16)
        part = jnp.dot(vt_ref[:, sub_rows], act_ref[sub_rows, :], preferred_element_type=F32)
        acc = part if acc is None else acc + part
    yt_ref[...] += acc

    @pl.when(j == pl.num_programs(1) - 1)
    def _():
        gate = ada_ref[:, 5 * D_MODEL:6 * D_MODEL]
        z = ALPHA * x_ref[...] + gate * yt_ref[...].T
        o_ref[...] = _layer_norm(z) * g_ref[...] + b_ref[...]


def _peer_call(x, ada, cond_base, cond_span, layer, wq_t, k1, k2, u, v_t, ln_g, ln_b):
    t = x.shape[0]
    tt = PEER_TOKEN_TILE
    tiles_per_cond = cond_span // tt
    eb = PEER_EXPERT_TILE
    row = pl.BlockSpec((1, D_MODEL), lambda i, j: (0, 0))
    per_head = pltpu.VMEM((PEER_HEADS, PEER_N_KEYS, tt), F32)
    return pl.pallas_call(
        _peer_kernel,
        out_shape=jax.ShapeDtypeStruct((t, D_MODEL), F32),
        grid=(t // tt, PEER_N_EXPERTS // eb),
        in_specs=[
            pl.BlockSpec((tt, D_MODEL), lambda i, j: (i, 0)),
            pl.BlockSpec((None, 1, ADA_CHUNKS * D_MODEL),
                         lambda i, j: (cond_base + i // tiles_per_cond, 0, 0)),
            pl.BlockSpec((PEER_HEADS * PEER_KEY_DIM, D_MODEL), lambda i, j: (0, 0)),
            pl.BlockSpec((PEER_N_KEYS, PEER_KEY_DIM // 2), lambda i, j: (0, 0)),
            pl.BlockSpec((PEER_N_KEYS, PEER_KEY_DIM // 2), lambda i, j: (0, 0)),
            pl.BlockSpec((None, eb, D_MODEL), lambda i, j: (layer, j, 0)),
            pl.BlockSpec((None, D_MODEL, eb), lambda i, j: (layer, 0, j)),
            row, row,
        ],
        out_specs=pl.BlockSpec((tt, D_MODEL), lambda i, j: (i, 0)),
        scratch_shapes=[
            pltpu.VMEM((tt, D_MODEL), BF16),
            pltpu.VMEM((D_MODEL, tt), BF16),
            pltpu.VMEM((D_MODEL, tt), F32),
            pltpu.VMEM((eb, tt), BF16),
            per_head, per_head, per_head, per_head,
        ],
        compiler_params=_params("parallel", "arbitrary"),
        name="peer",
    )(x, ada, wq_t, k1, k2, u, v_t, ln_g, ln_b)


def _rope_tables(seq_len):
    t = np.arange(seq_len)
    row = (t // GRID_W).astype(np.float64)
    col = (t % GRID_W).astype(np.float64)
    nf = HEAD_DIM // 4
    inv = ROPE_THETA ** (-np.arange(nf, dtype=np.float64) / nf)
    d = np.arange(HEAD_DIM)
    pos = np.where(d[None, :] < HEAD_DIM // 2, row[:, None], col[:, None])
    ang = pos * inv[d % nf][None, :]
    sign = np.where((d % (HEAD_DIM // 2)) < nf, -1.0, 1.0)
    cos = np.tile(np.cos(ang), (1, ATTN_HEADS))
    sin = np.tile(np.sin(ang) * sign[None, :], (1, ATTN_HEADS))
    return jnp.asarray(cos, F32), jnp.asarray(sin, F32)


def _head_mean_matrix():
    m = np.kron(np.eye(ATTN_HEADS), np.full((HEAD_DIM, HEAD_DIM), 1.0 / HEAD_DIM))
    return jnp.asarray(m, BF16)


def _block_diag(w):
    g, a, b = w.shape
    out = jnp.zeros((g * a, g * b), w.dtype)
    for i in range(g):
        out = out.at[i * a:(i + 1) * a, i * b:(i + 1) * b].set(w[i])
    return out


def _trunk_layer(x, ada, cond_base, batch, seq_len, lw, ctx_cache):
    rope = ctx_cache is not None
    if rope:
        cos_t, sin_t = _rope_tables(seq_len)
    else:
        cos_t = jnp.ones((seq_len, 256), F32)
        sin_t = jnp.zeros((seq_len, 256), F32)
    cond_span = seq_len if rope else batch * seq_len
    aq, ak, av, fx, nq, nk, nv, px = _inproj_call(
        x, ada, cond_base, cond_span, seq_len, lw["w_in"], lw["gq"], lw["gk"], lw["head_mean"],
        cos_t, sin_t, rope)
    if ctx_cache is None:
        a_out, n_out = _ctx_attn_call(aq, ak, av, nq, nk, nv, batch, seq_len)
    else:
        ck_a, cv_a, ck_n, cv_n = ctx_cache
        a_out = _lat_attn_call(aq, ak, av, ck_a, cv_a, batch, seq_len)
        n_out = _na_call(nq, nk, nv, ck_n, cv_n, lw["na_bias"], batch, seq_len)
    f_out = _fourier_call(fx, lw["w_fourier"], batch, seq_len)
    p_out = _pool_call(px, lw["w_pool"], lw["pool_scale"], batch, seq_len)
    x = _outproj_call(a_out, f_out, n_out, p_out, x, ada, cond_base, cond_span,
                      lw["w_out"], lw["ln1_g"], lw["ln1_b"])
    x = _peer_call(x, ada, cond_base, cond_span, lw["layer"], lw["wq_t"], lw["k1"], lw["k2"],
                   lw["u"], lw["v_t"], lw["ln2_g"], lw["ln2_b"])
    return x, (ak, av, nk, nv)


def kernel(x_prompt, x_sample, cache_attn_k, cache_attn_v, cache_na_k, cache_na_v, c, c_ctx,
           w_ada, b_ada, w_in, q_norm_g, k_norm_g, w_fourier, na_bias, w_pool, pool_scale,
           w_out, ln1_g, ln1_b, ln2_g, ln2_b, peer_wq, peer_k1, peer_k2, peer_u, peer_v):
    batch, seq, d = x_prompt.shape
    dec_batch, dec_seq, _ = x_sample.shape
    xp = x_prompt.reshape(batch * seq, d)
    xs = x_sample.reshape(dec_batch * dec_seq, d)

    cond = jnp.zeros((COND_ROWS, d), F32).at[0].set(c_ctx).at[1:1 + dec_batch].set(c)
    ada_all = _ada_call(cond, w_ada, b_ada)
    head_mean = _head_mean_matrix()
    rows = dec_seq // GRID_W

    def heads_first(cache):
        return cache.transpose(1, 0, 3, 2, 4)

    cak, cav = heads_first(cache_attn_k), heads_first(cache_attn_v)
    cnk, cnv = heads_first(cache_na_k), heads_first(cache_na_v)

    u_all = peer_u.astype(BF16)
    v_t_all = jnp.swapaxes(peer_v.astype(BF16), 1, 2)

    new_kv = [[], [], [], []]
    for l in range(DEPTH):
        lw = {
            "layer": l,
            "w_in": w_in[l].astype(BF16),
            "gq": jnp.tile(q_norm_g[l], ATTN_HEADS).reshape(1, -1),
            "gk": jnp.tile(k_norm_g[l], ATTN_KV_HEADS).reshape(1, -1),
            "head_mean": head_mean,
            "w_fourier": w_fourier[l].astype(BF16),
            "na_bias": _na_bias_patterns(na_bias[l], rows),
            "w_pool": _block_diag(w_pool[l]).astype(BF16),
            "pool_scale": pool_scale[l].reshape(1, -1),
            "w_out": w_out[l].astype(BF16),
            "ln1_g": ln1_g[l].reshape(1, -1), "ln1_b": ln1_b[l].reshape(1, -1),
            "ln2_g": ln2_g[l].reshape(1, -1), "ln2_b": ln2_b[l].reshape(1, -1),
            "wq_t": peer_wq[l].T.astype(BF16),
            "k1": peer_k1[l].astype(BF16), "k2": peer_k2[l].astype(BF16),
            "u": u_all,
            "v_t": v_t_all,
        }
        ada = ada_all[l].reshape(COND_ROWS, 1, ADA_CHUNKS * d)
        xp, kv = _trunk_layer(xp, ada, 0, batch, seq, lw, None)
        for dst, arr in zip(new_kv, kv):
            heads = arr.shape[0]
            dst.append(arr.reshape(heads, batch, seq, HEAD_DIM).transpose(1, 2, 0, 3))
        xs, _ = _trunk_layer(xs, ada, 1, dec_batch, dec_seq, lw,
                             (cak[l], cav[l], cnk[l], cnv[l]))
    outs = [jnp.stack(lst, axis=1) for lst in new_kv]
    return (xp.reshape(batch, seq, d), xs.reshape(dec_batch, dec_seq, d), *outs)
```

```python
import functools
import math

import numpy as np
import jax
import jax.numpy as jnp
from jax import lax
from jax.experimental import pallas as pl
from jax.experimental.pallas import tpu as pltpu

D_MODEL = 1024
DEPTH = 2
GRID_W = 64
HEAD_DIM = 64
GROUP_WIDTH = 256
ATTN_HEADS = 4
ATTN_KV_HEADS = 2
NA_HEADS = 4
NA_KH = 8
NA_KW = 16
FOURIER_HEADS = 4
POOL_WINDOWS = (2, 4, 8, 16)
POOL_GROUP_W = GROUP_WIDTH // len(POOL_WINDOWS)
ROPE_THETA = 10000.0
ATTN_SCALE = HEAD_DIM ** -0.5
PEER_HEADS = 8
PEER_N_KEYS = 128
PEER_N_EXPERTS = PEER_N_KEYS * PEER_N_KEYS
PEER_KEY_DIM = 256
PEER_TOPK = 16
ALPHA = (2 * DEPTH) ** 0.25
LN_EPS = 1e-6
RMS_EPS = 1e-6
ADA_CHUNKS = 6
P_IN = 1792
COND_ROWS = 8

TOKEN_TILE = 256
ATTN_Q_TILE = 512
DFT_ROW_TILE = 512
NA_GROUP_ROWS = 4
PEER_TOKEN_TILE = 512
PEER_EXPERT_TILE = 2048
PEER_SUB_TILE = 256
PEER_LANE_CHUNK = 128
ADA_COL_TILE = 1024
NEG_BIG = -1e30
VMEM_LIMIT = 56 * 1024 * 1024

F32 = jnp.float32
BF16 = jnp.bfloat16


def _params(*sem):
    return pltpu.CompilerParams(dimension_semantics=sem, vmem_limit_bytes=VMEM_LIMIT)


def _dot(a, b):
    return jnp.dot(a.astype(BF16), b.astype(BF16), preferred_element_type=F32)


def _dot_nt(a, b):
    return lax.dot_general(a.astype(BF16), b.astype(BF16), (((1,), (1,)), ((), ())),
                           preferred_element_type=F32)


def _layer_norm(x):
    mu = jnp.mean(x, axis=-1, keepdims=True)
    xc = x - mu
    var = jnp.mean(xc * xc, axis=-1, keepdims=True)
    return xc * lax.rsqrt(var + LN_EPS)


def _ada_kernel(cond_ref, w_ref, b_ref, o_ref):
    c = cond_ref[...]
    s = c / (1.0 + jnp.exp(-c))
    o_ref[...] = jnp.dot(s, w_ref[...], precision=lax.Precision.HIGHEST,
                         preferred_element_type=F32) + b_ref[...]


def _ada_call(cond, w_ada, b_ada):
    n = ADA_CHUNKS * D_MODEL
    return pl.pallas_call(
        _ada_kernel,
        out_shape=jax.ShapeDtypeStruct((DEPTH, COND_ROWS, n), F32),
        grid=(DEPTH, n // ADA_COL_TILE),
        in_specs=[
            pl.BlockSpec((COND_ROWS, D_MODEL), lambda l, j: (0, 0)),
            pl.BlockSpec((None, D_MODEL, ADA_COL_TILE), lambda l, j: (l, 0, j)),
            pl.BlockSpec((None, 1, ADA_COL_TILE), lambda l, j: (l, 0, j)),
        ],
        out_specs=pl.BlockSpec((None, COND_ROWS, ADA_COL_TILE), lambda l, j: (l, 0, j)),
        compiler_params=_params("parallel", "arbitrary"),
        name="ada_proj",
    )(cond, w_ada, b_ada.reshape(DEPTH, 1, n))


def _head_mean_sq(x, g):
    sq = x * x
    hi = sq.astype(BF16)
    lo = (sq - hi.astype(F32)).astype(BF16)
    return (jnp.dot(hi, g, preferred_element_type=F32)
            + jnp.dot(lo, g, preferred_element_type=F32))


def _rotary(x, cos, sin_signed):
    w = x.shape[-1]
    fwd = pltpu.roll(x, w - 16, axis=1)
    bwd = pltpu.roll(x, 16, axis=1)
    lane = lax.broadcasted_iota(jnp.int32, x.shape, 1)
    partner = jnp.where((lane % 32) < 16, fwd, bwd)
    return x * cos + partner * sin_signed


def _inproj_kernel(x_ref, ada_ref, w_ref, gq_ref, gk_ref, hm_ref, cos_ref, sin_ref,
                   aq_ref, ak_ref, av_ref, fx_ref, nq_ref, nk_ref, nv_ref, px_ref, *, rope):
    x = x_ref[...]
    sh = ada_ref[:, 0:D_MODEL]
    sc = ada_ref[:, D_MODEL:2 * D_MODEL]
    h = _layer_norm(x) * (1.0 + sc) + sh
    proj = _dot(h, w_ref[...])
    aq = proj[:, 0:256]
    ak = proj[:, 256:384]
    hm = hm_ref[...]
    aq = aq * lax.rsqrt(_head_mean_sq(aq, hm) + RMS_EPS) * gq_ref[...]
    ak = ak * lax.rsqrt(_head_mean_sq(ak, hm[0:128, 0:128]) + RMS_EPS) * gk_ref[...]
    if rope:
        cos = cos_ref[...]
        sin = sin_ref[...]
        aq = _rotary(aq, cos, sin)
        ak = _rotary(ak, cos[:, 0:128], sin[:, 0:128])
    for hd in range(ATTN_HEADS):
        aq_ref[hd] = aq[:, hd * 64:(hd + 1) * 64]
    for hd in range(ATTN_KV_HEADS):
        ak_ref[hd] = ak[:, hd * 64:(hd + 1) * 64]
        av_ref[hd] = proj[:, 384 + hd * 64:384 + (hd + 1) * 64]
    fx_ref[...] = proj[:, 512:768]
    for hd in range(NA_HEADS):
        nq_ref[hd] = proj[:, 768 + hd * 64:768 + (hd + 1) * 64]
        nk_ref[hd] = proj[:, 1024 + hd * 64:1024 + (hd + 1) * 64]
        nv_ref[hd] = proj[:, 1280 + hd * 64:1280 + (hd + 1) * 64]
    px_ref[...] = proj[:, 1536:1792]


def _inproj_call(x, ada, cond_base, cond_span, seq_len, w_in, gq, gk, head_mean, cos_t, sin_t,
                 rope):
    t = x.shape[0]
    tm = TOKEN_TILE
    tiles_per_seq = seq_len // tm
    tiles_per_cond = cond_span // tm
    hm = lambda n: jax.ShapeDtypeStruct((n, t, HEAD_DIM), F32)
    hspec = lambda n: pl.BlockSpec((n, tm, HEAD_DIM), lambda i: (0, i, 0))
    const = lambda shape: pl.BlockSpec(shape, lambda i: tuple(0 for _ in shape))
    return pl.pallas_call(
        functools.partial(_inproj_kernel, rope=rope),
        out_shape=(hm(4), hm(2), hm(2), jax.ShapeDtypeStruct((t, 256), F32),
                   hm(4), hm(4), hm(4), jax.ShapeDtypeStruct((t, 256), F32)),
        grid=(t // tm,),
        in_specs=[
            pl.BlockSpec((tm, D_MODEL), lambda i: (i, 0)),
            pl.BlockSpec((None, 1, ADA_CHUNKS * D_MODEL),
                         lambda i: (cond_base + i // tiles_per_cond, 0, 0)),
            const((D_MODEL, P_IN)),
            const((1, 256)), const((1, 128)), const((256, 256)),
            pl.BlockSpec((tm, 256), lambda i: (i % tiles_per_seq, 0)),
            pl.BlockSpec((tm, 256), lambda i: (i % tiles_per_seq, 0)),
        ],
        out_specs=(hspec(4), hspec(2), hspec(2), pl.BlockSpec((tm, 256), lambda i: (i, 0)),
                   hspec(4), hspec(4), hspec(4), pl.BlockSpec((tm, 256), lambda i: (i, 0))),
        compiler_params=_params("parallel"),
        name="inproj_rope" if rope else "inproj",
    )(x, ada, w_in, gq, gk, head_mean, cos_t, sin_t)


def _softmax_attend(q, k, v):
    s = _dot_nt(q * ATTN_SCALE, k)
    m = jnp.max(s, axis=-1, keepdims=True)
    p = jnp.exp(s - m)
    return _dot(p, v) / jnp.sum(p, axis=-1, keepdims=True)


def _ctx_attn_kernel(aq_ref, ak_ref, av_ref, nq_ref, nk_ref, nv_ref, a_ref, n_ref):
    outs = [_softmax_attend(aq_ref[hd], ak_ref[hd // 2], av_ref[hd // 2])
            for hd in range(ATTN_HEADS)]
    a_ref[...] = jnp.concatenate(outs, axis=-1)
    outs = [_softmax_attend(nq_ref[hd], nk_ref[hd], nv_ref[hd]) for hd in range(NA_HEADS)]
    n_ref[...] = jnp.concatenate(outs, axis=-1)


def _ctx_attn_call(aq, ak, av, nq, nk, nv, batch, seq_len):
    t = batch * seq_len
    hspec = lambda n: pl.BlockSpec((n, seq_len, HEAD_DIM), lambda b: (0, b, 0))
    ospec = pl.BlockSpec((seq_len, 256), lambda b: (b, 0))
    return pl.pallas_call(
        _ctx_attn_kernel,
        out_shape=(jax.ShapeDtypeStruct((t, 256), F32), jax.ShapeDtypeStruct((t, 256), F32)),
        grid=(batch,),
        in_specs=[hspec(4), hspec(2), hspec(2), hspec(4), hspec(4), hspec(4)],
        out_specs=(ospec, ospec),
        compiler_params=_params("parallel"),
        name="ctx_attention",
    )(aq, ak, av, nq, nk, nv)


def _lat_attn_kernel(q_ref, k_ref, v_ref, kc_ref, vc_ref, o_ref):
    k = k_ref[...]
    v = v_ref[...]
    kc = kc_ref[...]
    vc = vc_ref[...]
    outs = []
    for g in range(2):
        q = q_ref[g] * ATTN_SCALE
        s_l = _dot_nt(q, k)
        s_c = _dot_nt(q, kc)
        m = jnp.maximum(jnp.max(s_l, axis=-1, keepdims=True), jnp.max(s_c, axis=-1, keepdims=True))
        p_l = jnp.exp(s_l - m)
        p_c = jnp.exp(s_c - m)
        den = jnp.sum(p_l, axis=-1, keepdims=True) + jnp.sum(p_c, axis=-1, keepdims=True)
        outs.append((_dot(p_l, v) + _dot(p_c, vc)) / den)
    o_ref[...] = jnp.concatenate(outs, axis=-1)


def _lat_attn_call(aq, ak, av, kc, vc, batch, seq_len):
    t = batch * seq_len
    tq = ATTN_Q_TILE
    nqb = seq_len // tq
    return pl.pallas_call(
        _lat_attn_kernel,
        out_shape=jax.ShapeDtypeStruct((t, 256), F32),
        grid=(batch, ATTN_KV_HEADS, nqb),
        in_specs=[
            pl.BlockSpec((2, tq, HEAD_DIM), lambda b, g, i: (g, b * nqb + i, 0)),
            pl.BlockSpec((None, seq_len, HEAD_DIM), lambda b, g, i: (g, b, 0)),
            pl.BlockSpec((None, seq_len, HEAD_DIM), lambda b, g, i: (g, b, 0)),
            pl.BlockSpec((None, None, kc.shape[2], HEAD_DIM), lambda b, g, i: (b, g, 0, 0)),
            pl.BlockSpec((None, None, kc.shape[2], HEAD_DIM), lambda b, g, i: (b, g, 0, 0)),
        ],
        out_specs=pl.BlockSpec((tq, 128), lambda b, g, i: (b * nqb + i, g)),
        compiler_params=_params("parallel", "parallel", "arbitrary"),
        name="lat_attention",
    )(aq, ak, av, kc, vc)


def _na_geometry(rows):
    kh = min(NA_KH, rows)
    gr = NA_GROUP_ROWS
    kb = min(gr + kh, rows)
    starts, classes, tables = [], [], []
    for g in range(rows // gr):
        us = int(np.clip(gr * g - kh // 2, 0, rows - kb))
        rel = np.full((gr, kb), -1, np.int64)
        for dr in range(gr):
            r = gr * g + dr
            rs = int(np.clip(r - kh // 2, 0, rows - kh))
            assert us <= rs and rs + kh <= us + kb
            for k in range(kh):
                rel[dr, rs + k - us] = rs + k - r + (NA_KH - 1)
        for ci, tab in enumerate(tables):
            if np.array_equal(tab, rel):
                classes.append(ci)
                break
        else:
            classes.append(len(tables))
            tables.append(rel)
        starts.append(us)
    return kb, starts, classes, tables


def _na_kernel(start_ref, class_ref, q_ref, k_ref, v_ref, kc_ref, vc_ref, bias_ref, o_ref, *,
               rows, key_rows):
    gq = NA_GROUP_ROWS * GRID_W
    gk = key_rows * GRID_W

    def one_group(g, carry):
        q_off = pl.multiple_of(g * gq, gq)
        k_off = pl.multiple_of(start_ref[g] * GRID_W, GRID_W)
        cls = class_ref[g]
        outs = []
        for hd in range(2):
            q = q_ref[hd, pl.ds(q_off, gq), :] * ATTN_SCALE
            kw = k_ref[hd, pl.ds(k_off, gk), :]
            vw = v_ref[hd, pl.ds(k_off, gk), :]
            s_w = _dot_nt(q, kw) + bias_ref[hd, cls]
            s_c = _dot_nt(q, kc_ref[hd])
            m = jnp.maximum(jnp.max(s_w, axis=-1, keepdims=True),
                            jnp.max(s_c, axis=-1, keepdims=True))
            p_w = jnp.exp(s_w - m)
            p_c = jnp.exp(s_c - m)
            den = jnp.sum(p_w, axis=-1, keepdims=True) + jnp.sum(p_c, axis=-1, keepdims=True)
            outs.append((_dot(p_w, vw) + _dot(p_c, vc_ref[hd])) / den)
        o_ref[pl.ds(q_off, gq), :] = jnp.concatenate(outs, axis=-1)
        return carry

    lax.fori_loop(0, rows // NA_GROUP_ROWS, one_group, 0)


def _na_call(nq, nk, nv, kc, vc, bias, batch, seq_len):
    t = batch * seq_len
    rows = seq_len // GRID_W
    key_rows, starts, classes, _ = _na_geometry(rows)
    lspec = pl.BlockSpec((2, seq_len, HEAD_DIM), lambda b, hp, *_: (hp, b, 0))
    cspec = pl.BlockSpec((None, 2, kc.shape[2], HEAD_DIM), lambda b, hp, *_: (b, hp, 0, 0))
    return pl.pallas_call(
        functools.partial(_na_kernel, rows=rows, key_rows=key_rows),
        out_shape=jax.ShapeDtypeStruct((t, 256), F32),
        grid_spec=pltpu.PrefetchScalarGridSpec(
            num_scalar_prefetch=2,
            grid=(batch, NA_HEADS // 2),
            in_specs=[lspec, lspec, lspec, cspec, cspec,
                      pl.BlockSpec((2,) + bias.shape[1:], lambda b, hp, *_: (hp, 0, 0, 0))],
            out_specs=pl.BlockSpec((seq_len, 128), lambda b, hp, *_: (b, hp)),
        ),
        compiler_params=_params("parallel", "parallel"),
        name="neighbourhood_attention",
    )(jnp.asarray(starts, jnp.int32), jnp.asarray(classes, jnp.int32), nq, nk, nv, kc, vc, bias)


def _na_bias_patterns(bias_table, rows):
    key_rows, _, _, tables = _na_geometry(rows)
    rel = np.stack(tables)
    c = np.arange(GRID_W)
    cs = np.clip(c - NA_KW // 2, 0, GRID_W - NA_KW)
    kc = np.arange(GRID_W)
    inside = (kc[None, :] >= cs[:, None]) & (kc[None, :] < cs[:, None] + NA_KW)
    col_rel = np.clip(kc[None, :] - c[:, None] + (NA_KW - 1), 0, 2 * NA_KW - 2)
    col_pick = (col_rel[:, :, None] == np.arange(2 * NA_KW - 1)).astype(np.float32)
    blocks = jnp.einsum("hrs,cqs->hrcq", bias_table, col_pick, precision=lax.Precision.HIGHEST)
    blocks = jnp.where(inside[None, None], blocks, NEG_BIG)
    masked = jnp.full(blocks[:, 0].shape, NEG_BIG, F32)
    per_class = []
    for tab in rel:
        per_row = [jnp.concatenate([blocks[:, ro] if ro >= 0 else masked for ro in row], axis=-1)
                   for row in tab]
        per_class.append(jnp.concatenate(per_row, axis=1))
    return jnp.stack(per_class, axis=1)


def _fourier_kernel(x_ref, cl_ref, sl_ref, cc_ref, sc_ref, w_ref, o_ref, *, norm):
    x = x_ref[...]
    xc = _dot(x, cc_ref[...])
    xs = _dot(x, sc_ref[...])
    y = (_dot(cl_ref[...], xc) - _dot(sl_ref[...], xs)) * norm
    o_ref[...] = _dot(y, w_ref[...])


def _fourier_call(fx, w_fourier, batch, seq_len):
    t = batch * seq_len
    tm = min(DFT_ROW_TILE, seq_len)
    nmb = seq_len // tm
    j = np.arange(seq_len, dtype=np.float64)
    ang_l = 2.0 * np.pi * ((j[:, None] * j[None, :]) % seq_len) / seq_len
    hw = GROUP_WIDTH // FOURIER_HEADS
    c = np.arange(hw, dtype=np.float64)
    ang_c = 2.0 * np.pi * ((c[:, None] * c[None, :]) % hw) / hw
    eye = np.eye(FOURIER_HEADS)
    cl = jnp.asarray(np.cos(ang_l), F32)
    sl = jnp.asarray(np.sin(ang_l), F32)
    cc = jnp.asarray(np.kron(eye, np.cos(ang_c)), F32)
    sc = jnp.asarray(np.kron(eye, np.sin(ang_c)), F32)
    const = lambda shape: pl.BlockSpec(shape, lambda i, b: (0, 0))
    return pl.pallas_call(
        functools.partial(_fourier_kernel, norm=float((seq_len * hw) ** -0.5)),
        out_shape=jax.ShapeDtypeStruct((t, 256), F32),
        grid=(nmb, batch),
        in_specs=[
            pl.BlockSpec((seq_len, 256), lambda i, b: (b, 0)),
            pl.BlockSpec((tm, seq_len), lambda i, b: (i, 0)),
            pl.BlockSpec((tm, seq_len), lambda i, b: (i, 0)),
            const((256, 256)), const((256, 256)), const((256, 256)),
        ],
        out_specs=pl.BlockSpec((tm, 256), lambda i, b: (b * nmb + i, 0)),
        compiler_params=_params("parallel", "arbitrary"),
        name="fourier_mix",
    )(fx, cl, sl, cc, sc, w_fourier)


POOL_PAD = 8


def _pool_kernel(x_ref, w_ref, s_ref, o_ref, pad_ref, *, seq_len):
    x = x_ref[...]
    zeros = jnp.zeros((POOL_PAD, GROUP_WIDTH), F32)
    pad_ref[0:POOL_PAD, :] = zeros
    pad_ref[POOL_PAD + seq_len:2 * POOL_PAD + seq_len, :] = zeros
    pad_ref[POOL_PAD:POOL_PAD + seq_len, :] = x

    def shifted(off):
        return pad_ref[POOL_PAD + off:POOL_PAD + off + seq_len, :]

    t = lax.broadcasted_iota(jnp.int32, (seq_len, GROUP_WIDTH), 0)
    lane_group = lax.broadcasted_iota(jnp.int32, (seq_len, GROUP_WIDTH), 1) // POOL_GROUP_W
    acc = x
    done = 0
    pooled = jnp.zeros_like(x)
    for g, w in enumerate(POOL_WINDOWS):
        half = w // 2
        for off in list(range(-half, -done)) + list(range(max(done, 1), half)):
            acc = acc + shifted(off)
        done = half
        cnt = jnp.minimum(t + half, seq_len) - jnp.maximum(t - half, 0)
        pooled = jnp.where(lane_group == g, acc / cnt.astype(F32), pooled)
    o_ref[...] = _dot(pooled - x, w_ref[...]) * s_ref[...]


def _pool_call(px, w_pool_bd, pool_scale, batch, seq_len):
    t = batch * seq_len
    return pl.pallas_call(
        functools.partial(_pool_kernel, seq_len=seq_len),
        out_shape=jax.ShapeDtypeStruct((t, 256), F32),
        grid=(batch,),
        in_specs=[
            pl.BlockSpec((seq_len, 256), lambda b: (b, 0)),
            pl.BlockSpec((256, 256), lambda b: (0, 0)),
            pl.BlockSpec((1, 256), lambda b: (0, 0)),
        ],
        out_specs=pl.BlockSpec((seq_len, 256), lambda b: (b, 0)),
        scratch_shapes=[pltpu.VMEM((seq_len + 2 * POOL_PAD, GROUP_WIDTH), F32)],
        compiler_params=_params("parallel"),
        name="multiscale_pool",
    )(px, w_pool_bd, pool_scale)


def _outproj_kernel(a_ref, f_ref, n_ref, p_ref, x_ref, ada_ref, w_ref, g_ref, b_ref, o_ref):
    mix = (_dot(a_ref[...], w_ref[0:256, :]) + _dot(f_ref[...], w_ref[256:512, :])
           + _dot(n_ref[...], w_ref[512:768, :]) + _dot(p_ref[...], w_ref[768:1024, :]))
    gate = ada_ref[:, 2 * D_MODEL:3 * D_MODEL]
    z = ALPHA * x_ref[...] + gate * mix
    o_ref[...] = _layer_norm(z) * g_ref[...] + b_ref[...]


def _outproj_call(a_out, f_out, n_out, p_out, x, ada, cond_base, cond_span, w_out, ln_g, ln_b):
    t = x.shape[0]
    tm = TOKEN_TILE
    tiles_per_cond = cond_span // tm
    part = pl.BlockSpec((tm, 256), lambda i: (i, 0))
    row = pl.BlockSpec((1, D_MODEL), lambda i: (0, 0))
    return pl.pallas_call(
        _outproj_kernel,
        out_shape=jax.ShapeDtypeStruct((t, D_MODEL), F32),
        grid=(t // tm,),
        in_specs=[part, part, part, part,
                  pl.BlockSpec((tm, D_MODEL), lambda i: (i, 0)),
                  pl.BlockSpec((None, 1, ADA_CHUNKS * D_MODEL),
                               lambda i: (cond_base + i // tiles_per_cond, 0, 0)),
                  pl.BlockSpec((D_MODEL, D_MODEL), lambda i: (0, 0)), row, row],
        out_specs=pl.BlockSpec((tm, D_MODEL), lambda i: (i, 0)),
        compiler_params=_params("parallel"),
        name="outproj_postnorm",
    )(a_out, f_out, n_out, p_out, x, ada, w_out, ln_g, ln_b)


def _sorting_network(n):
    pairs = []
    p = 1
    while p < n:
        k = p
        while k >= 1:
            for j in range(k % p, n - k, 2 * k):
                for i in range(min(k, n - j - k)):
                    if (i + j) // (2 * p) == (i + j + k) // (2 * p):
                        pairs.append((i + j, i + j + k))
            k //= 2
        p *= 2
    return pairs


def _sort_descending(rows):
    rows = list(rows)
    for a, b in _sorting_network(len(rows)):
        rows[a], rows[b] = jnp.maximum(rows[a], rows[b]), jnp.minimum(rows[a], rows[b])
    return rows


def _sublane_max_all(x):
    x = jnp.maximum(x, pltpu.roll(x, 4, axis=0))
    x = jnp.maximum(x, pltpu.roll(x, 2, axis=0))
    return jnp.maximum(x, pltpu.roll(x, 1, axis=0))


def _extract_top(groups, count):
    groups = [list(g) for g in groups]
    out = []
    for it in range(count):
        head = groups[0][0]
        for g in groups[1:]:
            head = jnp.maximum(head, g[0])
        m = _sublane_max_all(head)
        out.append(m)
        remaining = count - it - 1
        if remaining == 0:
            break
        for g in groups:
            eq = g[0] == m
            keep = min(len(g), remaining)
            for i in range(keep):
                nxt = g[i + 1] if i + 1 < len(g) else jnp.full_like(m, -jnp.inf)
                g[i] = jnp.where(eq, nxt, g[i])
            del g[keep:]
    return out


def _peer_route(s1, s2):
    groups = PEER_N_KEYS // 8
    s1_rows = [s1[8 * i:8 * (i + 1)] for i in range(groups)]
    s2_rows = [s2[8 * i:8 * (i + 1)] for i in range(groups)]
    v1 = _extract_top([_sort_descending(s1_rows)], PEER_TOPK + 1)
    v2 = _extract_top([_sort_descending(s2_rows)], PEER_TOPK + 1)
    sub = lax.broadcasted_iota(jnp.int32, v1[0].shape, 0)
    lo = v1[0]
    hi = v1[8]
    for r in range(1, 8):
        lo = jnp.where(sub == r, v1[r], lo)
        hi = jnp.where(sub == r, v1[8 + r], hi)
    cand = [[lo + v2[b] for b in range(PEER_TOPK + 1)], [hi + v2[0]], [v1[PEER_TOPK] + v2[0]]]
    top = _extract_top(cand, PEER_TOPK + 1)
    den = jnp.ones_like(top[0])
    for tk in top[1:PEER_TOPK]:
        den = den + jnp.exp(tk - top[0])
    half_inv_den = 0.5 / den
    tau = 0.5 * (top[PEER_TOPK - 1] + top[PEER_TOPK])
    thr = [tau - r for r in s1_rows]
    coef = [jnp.exp(r - v1[0]) * half_inv_den for r in s1_rows]
    e2 = [jnp.exp(r - v2[0]) for r in s2_rows]
    return thr, coef, e2


def _peer_kernel(x_ref, ada_ref, wq_ref, k1_ref, k2_ref, u_ref, vt_ref, g_ref, b_ref, o_ref,
                 h_ref, ht_ref, yt_ref, act_ref, s2_ref, e2_ref, thr_ref, coef_ref):
    j = pl.program_id(1)
    tt = x_ref.shape[0]
    half = PEER_KEY_DIM // 2
    lanes = PEER_LANE_CHUNK

    @pl.when(j == 0)
    def _():
        sh = ada_ref[:, 3 * D_MODEL:4 * D_MODEL]
        sc = ada_ref[:, 4 * D_MODEL:5 * D_MODEL]
        h = _layer_norm(x_ref[...]) * (1.0 + sc) + sh
        h_ref[...] = h.astype(BF16)
        ht_ref[...] = h.T.astype(BF16)
        yt_ref[...] = jnp.zeros_like(yt_ref)

        def one_head(hd, carry):
            w_rows = wq_ref[pl.ds(pl.multiple_of(hd * PEER_KEY_DIM, PEER_KEY_DIM), PEER_KEY_DIM), :]
            q_t = lax.dot_general(w_rows, h_ref[...], (((1,), (1,)), ((), ())),
                                  preferred_element_type=F32)
            s1 = _dot(k1_ref[...], q_t[0:half])
            s2 = _dot(k2_ref[...], q_t[half:2 * half])
            s2_ref[hd] = s2
            for c in range(tt // lanes):
                cols = slice(c * lanes, (c + 1) * lanes)
                thr, coef, e2 = _peer_route(s1[:, cols], s2[:, cols])
                for i in range(PEER_N_KEYS // 8):
                    rows = pl.ds(8 * i, 8)
                    thr_ref[hd, rows, cols] = thr[i]
                    coef_ref[hd, rows, cols] = coef[i]
                    e2_ref[hd, rows, cols] = e2[i]
            return carry

        lax.fori_loop(0, PEER_HEADS, one_head, 0)

    keys_per_block = PEER_EXPERT_TILE // PEER_N_KEYS
    keys_per_sub = PEER_SUB_TILE // PEER_N_KEYS
    n_sub = PEER_EXPERT_TILE // PEER_SUB_TILE

    def up_proj(sub):
        return jnp.dot(u_ref[sub * PEER_SUB_TILE:(sub + 1) * PEER_SUB_TILE, :], ht_ref[...],
                       preferred_element_type=F32)

    acc = None
    a_next = up_proj(0)
    for sub in range(n_sub):
        sub_rows = slice(sub * PEER_SUB_TILE, (sub + 1) * PEER_SUB_TILE)
        a_sub = a_next
        if sub + 1 < n_sub:
            a_next = up_proj(sub + 1)
        for ii in range(keys_per_sub):
            i1 = j * keys_per_block + sub * keys_per_sub + ii
            rows = slice(ii * PEER_N_KEYS, (ii + 1) * PEER_N_KEYS)
            thr_rows = [thr_ref[hd, pl.ds(i1, 1), :] for hd in range(PEER_HEADS)]
            coef_rows = [coef_ref[hd, pl.ds(i1, 1), :] for hd in range(PEER_HEADS)]
            for c in range(tt // lanes):
                cols = slice(c * lanes, (c + 1) * lanes)
                w = None
                for hd in range(PEER_HEADS):
                    thr = thr_rows[hd][:, cols]
                    coef = coef_rows[hd][:, cols]
                    term = jnp.where(s2_ref[hd, :, cols] >= thr, e2_ref[hd, :, cols], 0.0) * coef
                    w = term if w is None else w + term
                a = a_sub[rows, cols]
                gelu2 = a * (1.0 + lax.erf(a * (2.0 ** -0.5)))
                act_ref[sub * PEER_SUB_TILE + ii * PEER_N_KEYS:
                        sub * PEER_SUB_TILE + (ii + 1) * PEER_N_KEYS, cols] = (w * gelu2).astype(BF16)
        part = jnp.dot(vt_ref[:, sub_rows], act_ref[sub_rows, :], preferred_element_type=F32)
        acc = part if acc is None else acc + part
    yt_ref[...] += acc

    @pl.when(j == pl.num_programs(1) - 1)
    def _():
        gate = ada_ref[:, 5 * D_MODEL:6 * D_MODEL]
        z = ALPHA * x_ref[...] + gate * yt_ref[...].T
        o_ref[...] = _layer_norm(z) * g_ref[...] + b_ref[...]


def _peer_call(x, ada, cond_base, cond_span, layer, wq_t, k1, k2, u, v_t, ln_g, ln_b):
    t = x.shape[0]
    tt = PEER_TOKEN_TILE
    tiles_per_cond = cond_span // tt
    eb = PEER_EXPERT_TILE
    row = pl.BlockSpec((1, D_MODEL), lambda i, j: (0, 0))
    per_head = pltpu.VMEM((PEER_HEADS, PEER_N_KEYS, tt), F32)
    return pl.pallas_call(
        _peer_kernel,
        out_shape=jax.ShapeDtypeStruct((t, D_MODEL), F32),
        grid=(t // tt, PEER_N_EXPERTS // eb),
        in_specs=[
            pl.BlockSpec((tt, D_MODEL), lambda i, j: (i, 0)),
            pl.BlockSpec((None, 1, ADA_CHUNKS * D_MODEL),
                         lambda i, j: (cond_base + i // tiles_per_cond, 0, 0)),
            pl.BlockSpec((PEER_HEADS * PEER_KEY_DIM, D_MODEL), lambda i, j: (0, 0)),
            pl.BlockSpec((PEER_N_KEYS, PEER_KEY_DIM // 2), lambda i, j: (0, 0)),
            pl.BlockSpec((PEER_N_KEYS, PEER_KEY_DIM // 2), lambda i, j: (0, 0)),
            pl.BlockSpec((None, eb, D_MODEL), lambda i, j: (layer, j, 0)),
            pl.BlockSpec((None, D_MODEL, eb), lambda i, j: (layer, 0, j)),
            row, row,
        ],
        out_specs=pl.BlockSpec((tt, D_MODEL), lambda i, j: (i, 0)),
        scratch_shapes=[
            pltpu.VMEM((tt, D_MODEL), BF16),
            pltpu.VMEM((D_MODEL, tt), BF16),
            pltpu.VMEM((D_MODEL, tt), F32),
            pltpu.VMEM((eb, tt), BF16),
            per_head, per_head, per_head, per_head,
        ],
        compiler_params=_params("parallel", "arbitrary"),
        name="peer",
    )(x, ada, wq_t, k1, k2, u, v_t, ln_g, ln_b)


def _rope_tables(seq_len):
    t = np.arange(seq_len)
    row = (t // GRID_W).astype(np.float64)
    col = (t % GRID_W).astype(np.float64)
    nf = HEAD_DIM // 4
    inv = ROPE_THETA ** (-np.arange(nf, dtype=np.float64) / nf)
    d = np.arange(HEAD_DIM)
    pos = np.where(d[None, :] < HEAD_DIM // 2, row[:, None], col[:, None])
    ang = pos * inv[d % nf][None, :]
    sign = np.where((d % (HEAD_DIM // 2)) < nf, -1.0, 1.0)
    cos = np.tile(np.cos(ang), (1, ATTN_HEADS))
    sin = np.tile(np.sin(ang) * sign[None, :], (1, ATTN_HEADS))
    return jnp.asarray(cos, F32), jnp.asarray(sin, F32)


def _head_mean_matrix():
    m = np.kron(np.eye(ATTN_HEADS), np.full((HEAD_DIM, HEAD_DIM), 1.0 / HEAD_DIM))
    return jnp.asarray(m, BF16)


def _block_diag(w):
    g, a, b = w.shape
    out = jnp.zeros((g * a, g * b), w.dtype)
    for i in range(g):
        out = out.at[i * a:(i + 1) * a, i * b:(i + 1) * b].set(w[i])
    return out


def _trunk_layer(x, ada, cond_base, batch, seq_len, lw, ctx_cache):
    rope = ctx_cache is not None
    if rope:
        cos_t, sin_t = _rope_tables(seq_len)
    else:
        cos_t = jnp.ones((seq_len, 256), F32)
        sin_t = jnp.zeros((seq_len, 256), F32)
    cond_span = seq_len if rope else batch * seq_len
    aq, ak, av, fx, nq, nk, nv, px = _inproj_call(
        x, ada, cond_base, cond_span, seq_len, lw["w_in"], lw["gq"], lw["gk"], lw["head_mean"],
        cos_t, sin_t, rope)
    if ctx_cache is None:
        a_out, n_out = _ctx_attn_call(aq, ak, av, nq, nk, nv, batch, seq_len)
    else:
        ck_a, cv_a, ck_n, cv_n = ctx_cache
        a_out = _lat_attn_call(aq, ak, av, ck_a, cv_a, batch, seq_len)
        n_out = _na_call(nq, nk, nv, ck_n, cv_n, lw["na_bias"], batch, seq_len)
    f_out = _fourier_call(fx, lw["w_fourier"], batch, seq_len)
    p_out = _pool_call(px, lw["w_pool"], lw["pool_scale"], batch, seq_len)
    x = _outproj_call(a_out, f_out, n_out, p_out, x, ada, cond_base, cond_span,
                      lw["w_out"], lw["ln1_g"], lw["ln1_b"])
    x = _peer_call(x, ada, cond_base, cond_span, lw["layer"], lw["wq_t"], lw["k1"], lw["k2"],
                   lw["u"], lw["v_t"], lw["ln2_g"], lw["ln2_b"])
    return x, (ak, av, nk, nv)


def kernel(x_prompt, x_sample, cache_attn_k, cache_attn_v, cache_na_k, cache_na_v, c, c_ctx,
           w_ada, b_ada, w_in, q_norm_g, k_norm_g, w_fourier, na_bias, w_pool, pool_scale,
           w_out, ln1_g, ln1_b, ln2_g, ln2_b, peer_wq, peer_k1, peer_k2, peer_u, peer_v):
    batch, seq, d = x_prompt.shape
    dec_batch, dec_seq, _ = x_sample.shape
    xp = x_prompt.reshape(batch * seq, d)
    xs = x_sample.reshape(dec_batch * dec_seq, d)

    cond = jnp.zeros((COND_ROWS, d), F32).at[0].set(c_ctx).at[1:1 + dec_batch].set(c)
    ada_all = _ada_call(cond, w_ada, b_ada)
    head_mean = _head_mean_matrix()
    rows = dec_seq // GRID_W

    def heads_first(cache):
        return cache.transpose(1, 0, 3, 2, 4)

    cak, cav = heads_first(cache_attn_k), heads_first(cache_attn_v)
    cnk, cnv = heads_first(cache_na_k), heads_first(cache_na_v)

    u_all = peer_u.astype(BF16)
    v_t_all = jnp.swapaxes(peer_v.astype(BF16), 1, 2)

    new_kv = [[], [], [], []]
    for l in range(DEPTH):
        lw = {
            "layer": l,
            "w_in": w_in[l].astype(BF16),
            "gq": jnp.tile(q_norm_g[l], ATTN_HEADS).reshape(1, -1),
            "gk": jnp.tile(k_norm_g[l], ATTN_KV_HEADS).reshape(1, -1),
            "head_mean": head_mean,
            "w_fourier": w_fourier[l].astype(BF16),
            "na_bias": _na_bias_patterns(na_bias[l], rows),
            "w_pool": _block_diag(w_pool[l]).astype(BF16),
            "pool_scale": pool_scale[l].reshape(1, -1),
            "w_out": w_out[l].astype(BF16),
            "ln1_g": ln1_g[l].reshape(1, -1), "ln1_b": ln1_b[l].reshape(1, -1),
            "ln2_g": ln2_g[l].reshape(1, -1), "ln2_b": ln2_b[l].reshape(1, -1),
            "wq_t": peer_wq[l].T.astype(BF16),
            "k1": peer_k1[l].astype(BF16), "k2": peer_k2[l].astype(BF16),
            "u": u_all,
            "v_t": v_t_all,
        }
        ada = ada_all[l].reshape(COND_ROWS, 1, ADA_CHUNKS * d)
        xp, kv = _trunk_layer(xp, ada, 0, batch, seq, lw, None)
        for dst, arr in zip(new_kv, kv):
            heads = arr.shape[0]
            dst.append(arr.reshape(heads, batch, seq, HEAD_DIM).transpose(1, 2, 0, 3))
        xs, _ = _trunk_layer(xs, ada, 1, dec_batch, dec_seq, lw,
                             (cak[l], cav[l], cnk[l], cnv[l]))
    outs = [jnp.stack(lst, axis=1) for lst in new_kv]
    return (xp.reshape(batch, seq, d), xs.reshape(dec_batch, dec_seq, d), *outs)
```

```python
import functools
import math

import numpy as np
import jax
import jax.numpy as jnp
from jax import lax
from jax.experimental import pallas as pl
from jax.experimental.pallas import tpu as pltpu

D_MODEL = 1024
DEPTH = 2
GRID_W = 64
HEAD_DIM = 64
GROUP_WIDTH = 256
ATTN_HEADS = 4
ATTN_KV_HEADS = 2
NA_HEADS = 4
NA_KH = 8
NA_KW = 16
FOURIER_HEADS = 4
POOL_WINDOWS = (2, 4, 8, 16)
POOL_GROUP_W = GROUP_WIDTH // len(POOL_WINDOWS)
ROPE_THETA = 10000.0
ATTN_SCALE = HEAD_DIM ** -0.5
PEER_HEADS = 8
PEER_N_KEYS = 128
PEER_N_EXPERTS = PEER_N_KEYS * PEER_N_KEYS
PEER_KEY_DIM = 256
PEER_TOPK = 16
ALPHA = (2 * DEPTH) ** 0.25
LN_EPS = 1e-6
RMS_EPS = 1e-6
ADA_CHUNKS = 6
P_IN = 1792
COND_ROWS = 8

TOKEN_TILE = 256
ATTN_Q_TILE = 512
DFT_ROW_TILE = 512
NA_GROUP_ROWS = 4
PEER_TOKEN_TILE = 512
PEER_EXPERT_TILE = 2048
PEER_SUB_TILE = 256
PEER_LANE_CHUNK = 128
ADA_COL_TILE = 1024
NEG_BIG = -1e30
VMEM_LIMIT = 56 * 1024 * 1024

F32 = jnp.float32
BF16 = jnp.bfloat16


def _params(*sem):
    return pltpu.CompilerParams(dimension_semantics=sem, vmem_limit_bytes=VMEM_LIMIT)


def _dot(a, b):
    return jnp.dot(a.astype(BF16), b.astype(BF16), preferred_element_type=F32)


def _dot_nt(a, b):
    return lax.dot_general(a.astype(BF16), b.astype(BF16), (((1,), (1,)), ((), ())),
                           preferred_element_type=F32)


def _layer_norm(x):
    mu = jnp.mean(x, axis=-1, keepdims=True)
    xc = x - mu
    var = jnp.mean(xc * xc, axis=-1, keepdims=True)
    return xc * lax.rsqrt(var + LN_EPS)


def _ada_kernel(cond_ref, w_ref, b_ref, o_ref):
    c = cond_ref[...]
    s = c / (1.0 + jnp.exp(-c))
    o_ref[...] = jnp.dot(s, w_ref[...], precision=lax.Precision.HIGHEST,
                         preferred_element_type=F32) + b_ref[...]


def _ada_call(cond, w_ada, b_ada):
    n = ADA_CHUNKS * D_MODEL
    return pl.pallas_call(
        _ada_kernel,
        out_shape=jax.ShapeDtypeStruct((DEPTH, COND_ROWS, n), F32),
        grid=(DEPTH, n // ADA_COL_TILE),
        in_specs=[
            pl.BlockSpec((COND_ROWS, D_MODEL), lambda l, j: (0, 0)),
            pl.BlockSpec((None, D_MODEL, ADA_COL_TILE), lambda l, j: (l, 0, j)),
            pl.BlockSpec((None, 1, ADA_COL_TILE), lambda l, j: (l, 0, j)),
        ],
        out_specs=pl.BlockSpec((None, COND_ROWS, ADA_COL_TILE), lambda l, j: (l, 0, j)),
        compiler_params=_params("parallel", "arbitrary"),
        name="ada_proj",
    )(cond, w_ada, b_ada.reshape(DEPTH, 1, n))


def _head_mean_sq(x, g):
    sq = x * x
    hi = sq.astype(BF16)
    lo = (sq - hi.astype(F32)).astype(BF16)
    return (jnp.dot(hi, g, preferred_element_type=F32)
            + jnp.dot(lo, g, preferred_element_type=F32))


def _rotary(x, cos, sin_signed):
    w = x.shape[-1]
    fwd = pltpu.roll(x, w - 16, axis=1)
    bwd = pltpu.roll(x, 16, axis=1)
    lane = lax.broadcasted_iota(jnp.int32, x.shape, 1)
    partner = jnp.where((lane % 32) < 16, fwd, bwd)
    return x * cos + partner * sin_signed


def _inproj_kernel(x_ref, ada_ref, w_ref, gq_ref, gk_ref, hm_ref, cos_ref, sin_ref,
                   aq_ref, ak_ref, av_ref, fx_ref, nq_ref, nk_ref, nv_ref, px_ref, *, rope):
    x = x_ref[...]
    sh = ada_ref[:, 0:D_MODEL]
    sc = ada_ref[:, D_MODEL:2 * D_MODEL]
    h = _layer_norm(x) * (1.0 + sc) + sh
    proj = _dot(h, w_ref[...])
    aq = proj[:, 0:256]
    ak = proj[:, 256:384]
    hm = hm_ref[...]
    aq = aq * lax.rsqrt(_head_mean_sq(aq, hm) + RMS_EPS) * gq_ref[...]
    ak = ak * lax.rsqrt(_head_mean_sq(ak, hm[0:128, 0:128]) + RMS_EPS) * gk_ref[...]
    if rope:
        cos = cos_ref[...]
        sin = sin_ref[...]
        aq = _rotary(aq, cos, sin)
        ak = _rotary(ak, cos[:, 0:128], sin[:, 0:128])
    for hd in range(ATTN_HEADS):
        aq_ref[hd] = aq[:, hd * 64:(hd + 1) * 64]
    for hd in range(ATTN_KV_HEADS):
        ak_ref[hd] = ak[:, hd * 64:(hd + 1) * 64]
        av_ref[hd] = proj[:, 384 + hd * 64:384 + (hd + 1) * 64]
    fx_ref[...] = proj[:, 512:768]
    for hd in range(NA_HEADS):
        nq_ref[hd] = proj[:, 768 + hd * 64:768 + (hd + 1) * 64]
        nk_ref[hd] = proj[:, 1024 + hd * 64:1024 + (hd + 1) * 64]
        nv_ref[hd] = proj[:, 1280 + hd * 64:1280 + (hd + 1) * 64]
    px_ref[...] = proj[:, 1536:1792]


def _inproj_call(x, ada, cond_base, cond_span, seq_len, w_in, gq, gk, head_mean, cos_t, sin_t,
                 rope):
    t = x.shape[0]
    tm = TOKEN_TILE
    tiles_per_seq = seq_len // tm
    tiles_per_cond = cond_span // tm
    hm = lambda n: jax.ShapeDtypeStruct((n, t, HEAD_DIM), F32)
    hspec = lambda n: pl.BlockSpec((n, tm, HEAD_DIM), lambda i: (0, i, 0))
    const = lambda shape: pl.BlockSpec(shape, lambda i: tuple(0 for _ in shape))
    return pl.pallas_call(
        functools.partial(_inproj_kernel, rope=rope),
        out_shape=(hm(4), hm(2), hm(2), jax.ShapeDtypeStruct((t, 256), F32),
                   hm(4), hm(4), hm(4), jax.ShapeDtypeStruct((t, 256), F32)),
        grid=(t // tm,),
        in_specs=[
            pl.BlockSpec((tm, D_MODEL), lambda i: (i, 0)),
            pl.BlockSpec((None, 1, ADA_CHUNKS * D_MODEL),
                         lambda i: (cond_base + i // tiles_per_cond, 0, 0)),
            const((D_MODEL, P_IN)),
            const((1, 256)), const((1, 128)), const((256, 256)),
            pl.BlockSpec((tm, 256), lambda i: (i % tiles_per_seq, 0)),
            pl.BlockSpec((tm, 256), lambda i: (i % tiles_per_seq, 0)),
        ],
        out_specs=(hspec(4), hspec(2), hspec(2), pl.BlockSpec((tm, 256), lambda i: (i, 0)),
                   hspec(4), hspec(4), hspec(4), pl.BlockSpec((tm, 256), lambda i: (i, 0))),
        compiler_params=_params("parallel"),
        name="inproj_rope" if rope else "inproj",
    )(x, ada, w_in, gq, gk, head_mean, cos_t, sin_t)


def _softmax_attend(q, k, v):
    s = _dot_nt(q * ATTN_SCALE, k)
    m = jnp.max(s, axis=-1, keepdims=True)
    p = jnp.exp(s - m)
    return _dot(p, v) / jnp.sum(p, axis=-1, keepdims=True)


def _ctx_attn_kernel(aq_ref, ak_ref, av_ref, nq_ref, nk_ref, nv_ref, a_ref, n_ref):
    outs = [_softmax_attend(aq_ref[hd], ak_ref[hd // 2], av_ref[hd // 2])
            for hd in range(ATTN_HEADS)]
    a_ref[...] = jnp.concatenate(outs, axis=-1)
    outs = [_softmax_attend(nq_ref[hd], nk_ref[hd], nv_ref[hd]) for hd in range(NA_HEADS)]
    n_ref[...] = jnp.concatenate(outs, axis=-1)


def _ctx_attn_call(aq, ak, av, nq, nk, nv, batch, seq_len):
    t = batch * seq_len
    hspec = lambda n: pl.BlockSpec((n, seq_len, HEAD_DIM), lambda b: (0, b, 0))
    ospec = pl.BlockSpec((seq_len, 256), lambda b: (b, 0))
    return pl.pallas_call(
        _ctx_attn_kernel,
        out_shape=(jax.ShapeDtypeStruct((t, 256), F32), jax.ShapeDtypeStruct((t, 256), F32)),
        grid=(batch,),
        in_specs=[hspec(4), hspec(2), hspec(2), hspec(4), hspec(4), hspec(4)],
        out_specs=(ospec, ospec),
        compiler_params=_params("parallel"),
        name="ctx_attention",
    )(aq, ak, av, nq, nk, nv)


def _lat_attn_kernel(q_ref, k_ref, v_ref, kc_ref, vc_ref, o_ref):
    k = k_ref[...]
    v = v_ref[...]
    kc = kc_ref[...]
    vc = vc_ref[...]
    outs = []
    for g in range(2):
        q = q_ref[g] * ATTN_SCALE
        s_l = _dot_nt(q, k)
        s_c = _dot_nt(q, kc)
        m = jnp.maximum(jnp.max(s_l, axis=-1, keepdims=True), jnp.max(s_c, axis=-1, keepdims=True))
        p_l = jnp.exp(s_l - m)
        p_c = jnp.exp(s_c - m)
        den = jnp.sum(p_l, axis=-1, keepdims=True) + jnp.sum(p_c, axis=-1, keepdims=True)
        outs.append((_dot(p_l, v) + _dot(p_c, vc)) / den)
    o_ref[...] = jnp.concatenate(outs, axis=-1)


def _lat_attn_call(aq, ak, av, kc, vc, batch, seq_len):
    t = batch * seq_len
    tq = ATTN_Q_TILE
    nqb = seq_len // tq
    return pl.pallas_call(
        _lat_attn_kernel,
        out_shape=jax.ShapeDtypeStruct((t, 256), F32),
        grid=(batch, ATTN_KV_HEADS, nqb),
        in_specs=[
            pl.BlockSpec((2, tq, HEAD_DIM), lambda b, g, i: (g, b * nqb + i, 0)),
            pl.BlockSpec((None, seq_len, HEAD_DIM), lambda b, g, i: (g, b, 0)),
            pl.BlockSpec((None, seq_len, HEAD_DIM), lambda b, g, i: (g, b, 0)),
            pl.BlockSpec((None, None, kc.shape[2], HEAD_DIM), lambda b, g, i: (b, g, 0, 0)),
            pl.BlockSpec((None, None, kc.shape[2], HEAD_DIM), lambda b, g, i: (b, g, 0, 0)),
        ],
        out_specs=pl.BlockSpec((tq, 128), lambda b, g, i: (b * nqb + i, g)),
        compiler_params=_params("parallel", "parallel", "arbitrary"),
        name="lat_attention",
    )(aq, ak, av, kc, vc)


def _na_geometry(rows):
    kh = min(NA_KH, rows)
    gr = NA_GROUP_ROWS
    kb = min(gr + kh, rows)
    starts, classes, tables = [], [], []
    for g in range(rows // gr):
        us = int(np.clip(gr * g - kh // 2, 0, rows - kb))
        rel = np.full((gr, kb), -1, np.int64)
        for dr in range(gr):
            r = gr * g + dr
            rs = int(np.clip(r - kh // 2, 0, rows - kh))
            assert us <= rs and rs + kh <= us + kb
            for k in range(kh):
                rel[dr, rs + k - us] = rs + k - r + (NA_KH - 1)
        for ci, tab in enumerate(tables):
            if np.array_equal(tab, rel):
                classes.append(ci)
                break
        else:
            classes.append(len(tables))
            tables.append(rel)
        starts.append(us)
    return kb, starts, classes, tables


def _na_kernel(start_ref, class_ref, q_ref, k_ref, v_ref, kc_ref, vc_ref, bias_ref, o_ref, *,
               rows, key_rows):
    gq = NA_GROUP_ROWS * GRID_W
    gk = key_rows * GRID_W

    def one_group(g, carry):
        q_off = pl.multiple_of(g * gq, gq)
        k_off = pl.multiple_of(start_ref[g] * GRID_W, GRID_W)
        cls = class_ref[g]
        outs = []
        for hd in range(2):
            q = q_ref[hd, pl.ds(q_off, gq), :] * ATTN_SCALE
            kw = k_ref[hd, pl.ds(k_off, gk), :]
            vw = v_ref[hd, pl.ds(k_off, gk), :]
            s_w = _dot_nt(q, kw) + bias_ref[hd, cls]
            s_c = _dot_nt(q, kc_ref[hd])
            m = jnp.maximum(jnp.max(s_w, axis=-1, keepdims=True),
                            jnp.max(s_c, axis=-1, keepdims=True))
            p_w = jnp.exp(s_w - m)
            p_c = jnp.exp(s_c - m)
            den = jnp.sum(p_w, axis=-1, keepdims=True) + jnp.sum(p_c, axis=-1, keepdims=True)
            outs.append((_dot(p_w, vw) + _dot(p_c, vc_ref[hd])) / den)
        o_ref[pl.ds(q_off, gq), :] = jnp.concatenate(outs, axis=-1)
        return carry

    lax.fori_loop(0, rows // NA_GROUP_ROWS, one_group, 0)


def _na_call(nq, nk, nv, kc, vc, bias, batch, seq_len):
    t = batch * seq_len
    rows = seq_len // GRID_W
    key_rows, starts, classes, _ = _na_geometry(rows)
    lspec = pl.BlockSpec((2, seq_len, HEAD_DIM), lambda b, hp, *_: (hp, b, 0))
    cspec = pl.BlockSpec((None, 2, kc.shape[2], HEAD_DIM), lambda b, hp, *_: (b, hp, 0, 0))
    return pl.pallas_call(
        functools.partial(_na_kernel, rows=rows, key_rows=key_rows),
        out_shape=jax.ShapeDtypeStruct((t, 256), F32),
        grid_spec=pltpu.PrefetchScalarGridSpec(
            num_scalar_prefetch=2,
            grid=(batch, NA_HEADS // 2),
            in_specs=[lspec, lspec, lspec, cspec, cspec,
                      pl.BlockSpec((2,) + bias.shape[1:], lambda b, hp, *_: (hp, 0, 0, 0))],
            out_specs=pl.BlockSpec((seq_len, 128), lambda b, hp, *_: (b, hp)),
        ),
        compiler_params=_params("parallel", "parallel"),
        name="neighbourhood_attention",
    )(jnp.asarray(starts, jnp.int32), jnp.asarray(classes, jnp.int32), nq, nk, nv, kc, vc, bias)


def _na_bias_patterns(bias_table, rows):
    key_rows, _, _, tables = _na_geometry(rows)
    rel = np.stack(tables)
    c = np.arange(GRID_W)
    cs = np.clip(c - NA_KW // 2, 0, GRID_W - NA_KW)
    kc = np.arange(GRID_W)
    inside = (kc[None, :] >= cs[:, None]) & (kc[None, :] < cs[:, None] + NA_KW)
    col_rel = np.clip(kc[None, :] - c[:, None] + (NA_KW - 1), 0, 2 * NA_KW - 2)
    col_pick = (col_rel[:, :, None] == np.arange(2 * NA_KW - 1)).astype(np.float32)
    blocks = jnp.einsum("hrs,cqs->hrcq", bias_table, col_pick, precision=lax.Precision.HIGHEST)
    blocks = jnp.where(inside[None, None], blocks, NEG_BIG)
    masked = jnp.full(blocks[:, 0].shape, NEG_BIG, F32)
    per_class = []
    for tab in rel:
        per_row = [jnp.concatenate([blocks[:, ro] if ro >= 0 else masked for ro in row], axis=-1)
                   for row in tab]
        per_class.append(jnp.concatenate(per_row, axis=1))
    return jnp.stack(per_class, axis=1)


def _fourier_kernel(x_ref, cl_ref, sl_ref, cc_ref, sc_ref, w_ref, o_ref, *, norm):
    x = x_ref[...]
    xc = _dot(x, cc_ref[...])
    xs = _dot(x, sc_ref[...])
    y = (_dot(cl_ref[...], xc) - _dot(sl_ref[...], xs)) * norm
    o_ref[...] = _dot(y, w_ref[...])


def _fourier_call(fx, w_fourier, batch, seq_len):
    t = batch * seq_len
    tm = min(DFT_ROW_TILE, seq_len)
    nmb = seq_len // tm
    j = np.arange(seq_len, dtype=np.float64)
    ang_l = 2.0 * np.pi * ((j[:, None] * j[None, :]) % seq_len) / seq_len
    hw = GROUP_WIDTH // FOURIER_HEADS
    c = np.arange(hw, dtype=np.float64)
    ang_c = 2.0 * np.pi * ((c[:, None] * c[None, :]) % hw) / hw
    eye = np.eye(FOURIER_HEADS)
    cl = jnp.asarray(np.cos(ang_l), F32)
    sl = jnp.asarray(np.sin(ang_l), F32)
    cc = jnp.asarray(np.kron(eye, np.cos(ang_c)), F32)
    sc = jnp.asarray(np.kron(eye, np.sin(ang_c)), F32)
    const = lambda shape: pl.BlockSpec(shape, lambda i, b: (0, 0))
    return pl.pallas_call(
        functools.partial(_fourier_kernel, norm=float((seq_len * hw) ** -0.5)),
        out_shape=jax.ShapeDtypeStruct((t, 256), F32),
        grid=(nmb, batch),
        in_specs=[
            pl.BlockSpec((seq_len, 256), lambda i, b: (b, 0)),
            pl.BlockSpec((tm, seq_len), lambda i, b: (i, 0)),
            pl.BlockSpec((tm, seq_len), lambda i, b: (i, 0)),
            const((256, 256)), const((256, 256)), const((256, 256)),
        ],
        out_specs=pl.BlockSpec((tm, 256), lambda i, b: (b * nmb + i, 0)),
        compiler_params=_params("parallel", "arbitrary"),
        name="fourier_mix",
    )(fx, cl, sl, cc, sc, w_fourier)


POOL_PAD = 8


def _pool_kernel(x_ref, w_ref, s_ref, o_ref, pad_ref, *, seq_len):
    x = x_ref[...]
    zeros = jnp.zeros((POOL_PAD, GROUP_WIDTH), F32)
    pad_ref[0:POOL_PAD, :] = zeros
    pad_ref[POOL_PAD + seq_len:2 * POOL_PAD + seq_len, :] = zeros
    pad_ref[POOL_PAD:POOL_PAD + seq_len, :] = x

    def shifted(off):
        return pad_ref[POOL_PAD + off:POOL_PAD + off + seq_len, :]

    t = lax.broadcasted_iota(jnp.int32, (seq_len, GROUP_WIDTH), 0)
    lane_group = lax.broadcasted_iota(jnp.int32, (seq_len, GROUP_WIDTH), 1) // POOL_GROUP_W
    acc = x
    done = 0
    pooled = jnp.zeros_like(x)
    for g, w in enumerate(POOL_WINDOWS):
        half = w // 2
        for off in list(range(-half, -done)) + list(range(max(done, 1), half)):
            acc = acc + shifted(off)
        done = half
        cnt = jnp.minimum(t + half, seq_len) - jnp.maximum(t - half, 0)
        pooled = jnp.where(lane_group == g, acc / cnt.astype(F32), pooled)
    o_ref[...] = _dot(pooled - x, w_ref[...]) * s_ref[...]


def _pool_call(px, w_pool_bd, pool_scale, batch, seq_len):
    t = batch * seq_len
    return pl.pallas_call(
        functools.partial(_pool_kernel, seq_len=seq_len),
        out_shape=jax.ShapeDtypeStruct((t, 256), F32),
        grid=(batch,),
        in_specs=[
            pl.BlockSpec((seq_len, 256), lambda b: (b, 0)),
            pl.BlockSpec((256, 256), lambda b: (0, 0)),
            pl.BlockSpec((1, 256), lambda b: (0, 0)),
        ],
        out_specs=pl.BlockSpec((seq_len, 256), lambda b: (b, 0)),
        scratch_shapes=[pltpu.VMEM((seq_len + 2 * POOL_PAD, GROUP_WIDTH), F32)],
        compiler_params=_params("parallel"),
        name="multiscale_pool",
    )(px, w_pool_bd, pool_scale)


def _outproj_kernel(a_ref, f_ref, n_ref, p_ref, x_ref, ada_ref, w_ref, g_ref, b_ref, o_ref):
    mix = (_dot(a_ref[...], w_ref[0:256, :]) + _dot(f_ref[...], w_ref[256:512, :])
           + _dot(n_ref[...], w_ref[512:768, :]) + _dot(p_ref[...], w_ref[768:1024, :]))
    gate = ada_ref[:, 2 * D_MODEL:3 * D_MODEL]
    z = ALPHA * x_ref[...] + gate * mix
    o_ref[...] = _layer_norm(z) * g_ref[...] + b_ref[...]


def _outproj_call(a_out, f_out, n_out, p_out, x, ada, cond_base, cond_span, w_out, ln_g, ln_b):
    t = x.shape[0]
    tm = TOKEN_TILE
    tiles_per_cond = cond_span // tm
    part = pl.BlockSpec((tm, 256), lambda i: (i, 0))
    row = pl.BlockSpec((1, D_MODEL), lambda i: (0, 0))
    return pl.pallas_call(
        _outproj_kernel,
        out_shape=jax.ShapeDtypeStruct((t, D_MODEL), F32),
        grid=(t // tm,),
        in_specs=[part, part, part, part,
                  pl.BlockSpec((tm, D_MODEL), lambda i: (i, 0)),
                  pl.BlockSpec((None, 1, ADA_CHUNKS * D_MODEL),
                               lambda i: (cond_base + i // tiles_per_cond, 0, 0)),
                  pl.BlockSpec((D_MODEL, D_MODEL), lambda i: (0, 0)), row, row],
        out_specs=pl.BlockSpec((tm, D_MODEL), lambda i: (i, 0)),
        compiler_params=_params("parallel"),
        name="outproj_postnorm",
    )(a_out, f_out, n_out, p_out, x, ada, w_out, ln_g, ln_b)


def _sorting_network(n):
    pairs = []
    p = 1
    while p < n:
        k = p
        while k >= 1:
            for j in range(k % p, n - k, 2 * k):
                for i in range(min(k, n - j - k)):
                    if (i + j) // (2 * p) == (i + j + k) // (2 * p):
                        pairs.append((i + j, i + j + k))
            k //= 2
        p *= 2
    return pairs


def _sort_descending(rows):
    rows = list(rows)
    for a, b in _sorting_network(len(rows)):
        rows[a], rows[b] = jnp.maximum(rows[a], rows[b]), jnp.minimum(rows[a], rows[b])
    return rows


def _sublane_max_all(x):
    x = jnp.maximum(x, pltpu.roll(x, 4, axis=0))
    x = jnp.maximum(x, pltpu.roll(x, 2, axis=0))
    return jnp.maximum(x, pltpu.roll(x, 1, axis=0))


def _extract_top(groups, count):
    groups = [list(g) for g in groups]
    out = []
    for it in range(count):
        head = groups[0][0]
        for g in groups[1:]:
            head = jnp.maximum(head, g[0])
        m = _sublane_max_all(head)
        out.append(m)
        remaining = count - it - 1
        if remaining == 0:
            break
        for g in groups:
            eq = g[0] == m
            keep = min(len(g), remaining)
            for i in range(keep):
                nxt = g[i + 1] if i + 1 < len(g) else jnp.full_like(m, -jnp.inf)
                g[i] = jnp.where(eq, nxt, g[i])
            del g[keep:]
    return out


def _peer_route(s1, s2):
    groups = PEER_N_KEYS // 8
    s1_rows = [s1[8 * i:8 * (i + 1)] for i in range(groups)]
    s2_rows = [s2[8 * i:8 * (i + 1)] for i in range(groups)]
    v1 = _extract_top([_sort_descending(s1_rows)], PEER_TOPK + 1)
    v2 = _extract_top([_sort_descending(s2_rows)], PEER_TOPK + 1)
    sub = lax.broadcasted_iota(jnp.int32, v1[0].shape, 0)
    lo = v1[0]
    hi = v1[8]
    for r in range(1, 8):
        lo = jnp.where(sub == r, v1[r], lo)
        hi = jnp.where(sub == r, v1[8 + r], hi)
    cand = [[lo + v2[b] for b in range(PEER_TOPK + 1)], [hi + v2[0]], [v1[PEER_TOPK] + v2[0]]]
    top = _extract_top(cand, PEER_TOPK + 1)
    den = jnp.ones_like(top[0])
    for tk in top[1:PEER_TOPK]:
        den = den + jnp.exp(tk - top[0])
    half_inv_den = 0.5 / den
    tau = 0.5 * (top[PEER_TOPK - 1] + top[PEER_TOPK])
    thr = [tau - r for r in s1_rows]
    coef = [jnp.exp(r - v1[0]) * half_inv_den for r in s1_rows]
    e2 = [jnp.exp(r - v2[0]) for r in s2_rows]
    return thr, coef, e2


def _peer_kernel(x_ref, ada_ref, wq_ref, k1_ref, k2_ref, u_ref, vt_ref, g_ref, b_ref, o_ref,
                 h_ref, ht_ref, yt_ref, act_ref, s2_ref, e2_ref, thr_ref, coef_ref):
    j = pl.program_id(1)
    tt = x_ref.shape[0]
    half = PEER_KEY_DIM // 2
    lanes = PEER_LANE_CHUNK

    @pl.when(j == 0)
    def _():
        sh = ada_ref[:, 3 * D_MODEL:4 * D_MODEL]
        sc = ada_ref[:, 4 * D_MODEL:5 * D_MODEL]
        h = _layer_norm(x_ref[...]) * (1.0 + sc) + sh
        h_ref[...] = h.astype(BF16)
        ht_ref[...] = h.T.astype(BF16)
        yt_ref[...] = jnp.zeros_like(yt_ref)

        def one_head(hd, carry):
            w_rows = wq_ref[pl.ds(pl.multiple_of(hd * PEER_KEY_DIM, PEER_KEY_DIM), PEER_KEY_DIM), :]
            q_t = lax.dot_general(w_rows, h_ref[...], (((1,), (1,)), ((), ())),
                                  preferred_element_type=F32)
            s1 = _dot(k1_ref[...], q_t[0:half])
            s2 = _dot(k2_ref[...], q_t[half:2 * half])
            s2_ref[hd] = s2
            for c in range(tt // lanes):
                cols = slice(c * lanes, (c + 1) * lanes)
                thr, coef, e2 = _peer_route(s1[:, cols], s2[:, cols])
                for i in range(PEER_N_KEYS // 8):
                    rows = pl.ds(8 * i, 8)
                    thr_ref[hd, rows, cols] = thr[i]
                    coef_ref[hd, rows, cols] = coef[i]
                    e2_ref[hd, rows, cols] = e2[i]
            return carry

        lax.fori_loop(0, PEER_HEADS, one_head, 0)

    keys_per_block = PEER_EXPERT_TILE // PEER_N_KEYS
    keys_per_sub = PEER_SUB_TILE // PEER_N_KEYS
    n_sub = PEER_EXPERT_TILE // PEER_SUB_TILE

    def up_proj(sub):
        return jnp.dot(u_ref[sub * PEER_SUB_TILE:(sub + 1) * PEER_SUB_TILE, :], ht_ref[...],
                       preferred_element_type=F32)

    acc = None
    a_next = up_proj(0)
    for sub in range(n_sub):
        sub_rows = slice(sub * PEER_SUB_TILE, (sub + 1) * PEER_SUB_TILE)
        a_sub = a_next
        if sub + 1 < n_sub:
            a_next = up_proj(sub + 1)
        for ii in range(keys_per_sub):
            i1 = j * keys_per_block + sub * keys_per_sub + ii
            rows = slice(ii * PEER_N_KEYS, (ii + 1) * PEER_N_KEYS)
            thr_rows = [thr_ref[hd, pl.ds(i1, 1), :] for hd in range(PEER_HEADS)]
            coef_rows = [coef_ref[hd, pl.ds(i1, 1), :] for hd in range(PEER_HEADS)]
            for c in range(tt // lanes):
                cols = slice(c * lanes, (c + 1) * lanes)
                w = None
                for hd in range(PEER_HEADS):
                    thr = thr_rows[hd][:, cols]
                    coef = coef_rows[hd][:, cols]
                    term = jnp.where(s2_ref[hd, :, cols] >= thr, e2_ref[hd, :, cols], 0.0) * coef
                    w = term if w is None else w + term
                a = a_sub[rows, cols]
                gelu2 = a * (1.0 + lax.erf(a * (2.0 ** -0.5)))
                act_ref[sub * PEER_SUB_TILE + ii * PEER_N_KEYS:
                        sub * PEER_SUB_TILE + (ii + 1) * PEER_N_KEYS, cols] = (w * gelu2).astype(BF16)
        part = lax.dot_general(vt_ref[sub_rows, :], act_ref[sub_rows, :],
                               (((0,), (0,)), ((), ())), preferred_element_type=F32)
        acc = part if acc is None else acc + part
    yt_ref[...] += acc

    @pl.when(j == pl.num_programs(1) - 1)
    def _():
        gate = ada_ref[:, 5 * D_MODEL:6 * D_MODEL]
        z = ALPHA * x_ref[...] + gate * yt_ref[...].T
        o_ref[...] = _layer_norm(z) * g_ref[...] + b_ref[...]


def _peer_call(x, ada, cond_base, cond_span, layer, wq_t, k1, k2, u, v_t, ln_g, ln_b):
    t = x.shape[0]
    tt = PEER_TOKEN_TILE
    tiles_per_cond = cond_span // tt
    eb = PEER_EXPERT_TILE
    row = pl.BlockSpec((1, D_MODEL), lambda i, j: (0, 0))
    per_head = pltpu.VMEM((PEER_HEADS, PEER_N_KEYS, tt), F32)
    return pl.pallas_call(
        _peer_kernel,
        out_shape=jax.ShapeDtypeStruct((t, D_MODEL), F32),
        grid=(t // tt, PEER_N_EXPERTS // eb),
        in_specs=[
            pl.BlockSpec((tt, D_MODEL), lambda i, j: (i, 0)),
            pl.BlockSpec((None, 1, ADA_CHUNKS * D_MODEL),
                         lambda i, j: (cond_base + i // tiles_per_cond, 0, 0)),
            pl.BlockSpec((PEER_HEADS * PEER_KEY_DIM, D_MODEL), lambda i, j: (0, 0)),
            pl.BlockSpec((PEER_N_KEYS, PEER_KEY_DIM // 2), lambda i, j: (0, 0)),
            pl.BlockSpec((PEER_N_KEYS, PEER_KEY_DIM // 2), lambda i, j: (0, 0)),
            pl.BlockSpec((None, eb, D_MODEL), lambda i, j: (layer, j, 0)),
            pl.BlockSpec((None, eb, D_MODEL), lambda i, j: (layer, j, 0)),
            row, row,
        ],
        out_specs=pl.BlockSpec((tt, D_MODEL), lambda i, j: (i, 0)),
        scratch_shapes=[
            pltpu.VMEM((tt, D_MODEL), BF16),
            pltpu.VMEM((D_MODEL, tt), BF16),
            pltpu.VMEM((D_MODEL, tt), F32),
            pltpu.VMEM((eb, tt), BF16),
            per_head, per_head, per_head, per_head,
        ],
        compiler_params=_params("parallel", "arbitrary"),
        name="peer",
    )(x, ada, wq_t, k1, k2, u, v_t, ln_g, ln_b)


def _rope_tables(seq_len):
    t = np.arange(seq_len)
    row = (t // GRID_W).astype(np.float64)
    col = (t % GRID_W).astype(np.float64)
    nf = HEAD_DIM // 4
    inv = ROPE_THETA ** (-np.arange(nf, dtype=np.float64) / nf)
    d = np.arange(HEAD_DIM)
    pos = np.where(d[None, :] < HEAD_DIM // 2, row[:, None], col[:, None])
    ang = pos * inv[d % nf][None, :]
    sign = np.where((d % (HEAD_DIM // 2)) < nf, -1.0, 1.0)
    cos = np.tile(np.cos(ang), (1, ATTN_HEADS))
    sin = np.tile(np.sin(ang) * sign[None, :], (1, ATTN_HEADS))
    return jnp.asarray(cos, F32), jnp.asarray(sin, F32)


def _head_mean_matrix():
    m = np.kron(np.eye(ATTN_HEADS), np.full((HEAD_DIM, HEAD_DIM), 1.0 / HEAD_DIM))
    return jnp.asarray(m, BF16)


def _block_diag(w):
    g, a, b = w.shape
    out = jnp.zeros((g * a, g * b), w.dtype)
    for i in range(g):
        out = out.at[i * a:(i + 1) * a, i * b:(i + 1) * b].set(w[i])
    return out


def _trunk_layer(x, ada, cond_base, batch, seq_len, lw, ctx_cache):
    rope = ctx_cache is not None
    if rope:
        cos_t, sin_t = _rope_tables(seq_len)
    else:
        cos_t = jnp.ones((seq_len, 256), F32)
        sin_t = jnp.zeros((seq_len, 256), F32)
    cond_span = seq_len if rope else batch * seq_len
    aq, ak, av, fx, nq, nk, nv, px = _inproj_call(
        x, ada, cond_base, cond_span, seq_len, lw["w_in"], lw["gq"], lw["gk"], lw["head_mean"],
        cos_t, sin_t, rope)
    if ctx_cache is None:
        a_out, n_out = _ctx_attn_call(aq, ak, av, nq, nk, nv, batch, seq_len)
    else:
        ck_a, cv_a, ck_n, cv_n = ctx_cache
        a_out = _lat_attn_call(aq, ak, av, ck_a, cv_a, batch, seq_len)
        n_out = _na_call(nq, nk, nv, ck_n, cv_n, lw["na_bias"], batch, seq_len)
    f_out = _fourier_call(fx, lw["w_fourier"], batch, seq_len)
    p_out = _pool_call(px, lw["w_pool"], lw["pool_scale"], batch, seq_len)
    x = _outproj_call(a_out, f_out, n_out, p_out, x, ada, cond_base, cond_span,
                      lw["w_out"], lw["ln1_g"], lw["ln1_b"])
    x = _peer_call(x, ada, cond_base, cond_span, lw["layer"], lw["wq_t"], lw["k1"], lw["k2"],
                   lw["u"], lw["v_t"], lw["ln2_g"], lw["ln2_b"])
    return x, (ak, av, nk, nv)


def kernel(x_prompt, x_sample, cache_attn_k, cache_attn_v, cache_na_k, cache_na_v, c, c_ctx,
           w_ada, b_ada, w_in, q_norm_g, k_norm_g, w_fourier, na_bias, w_pool, pool_scale,
           w_out, ln1_g, ln1_b, ln2_g, ln2_b, peer_wq, peer_k1, peer_k2, peer_u, peer_v):
    batch, seq, d = x_prompt.shape
    dec_batch, dec_seq, _ = x_sample.shape
    xp = x_prompt.reshape(batch * seq, d)
    xs = x_sample.reshape(dec_batch * dec_seq, d)

    cond = jnp.zeros((COND_ROWS, d), F32).at[0].set(c_ctx).at[1:1 + dec_batch].set(c)
    ada_all = _ada_call(cond, w_ada, b_ada)
    head_mean = _head_mean_matrix()
    rows = dec_seq // GRID_W

    def heads_first(cache):
        return cache.transpose(1, 0, 3, 2, 4)

    cak, cav = heads_first(cache_attn_k), heads_first(cache_attn_v)
    cnk, cnv = heads_first(cache_na_k), heads_first(cache_na_v)

    u_all = peer_u.astype(BF16)
    v_t_all = peer_v.astype(BF16)

    new_kv = [[], [], [], []]
    for l in range(DEPTH):
        lw = {
            "layer": l,
            "w_in": w_in[l].astype(BF16),
            "gq": jnp.tile(q_norm_g[l], ATTN_HEADS).reshape(1, -1),
            "gk": jnp.tile(k_norm_g[l], ATTN_KV_HEADS).reshape(1, -1),
            "head_mean": head_mean,
            "w_fourier": w_fourier[l].astype(BF16),
            "na_bias": _na_bias_patterns(na_bias[l], rows),
            "w_pool": _block_diag(w_pool[l]).astype(BF16),
            "pool_scale": pool_scale[l].reshape(1, -1),
            "w_out": w_out[l].astype(BF16),
            "ln1_g": ln1_g[l].reshape(1, -1), "ln1_b": ln1_b[l].reshape(1, -1),
            "ln2_g": ln2_g[l].reshape(1, -1), "ln2_b": ln2_b[l].reshape(1, -1),
            "wq_t": peer_wq[l].T.astype(BF16),
            "k1": peer_k1[l].astype(BF16), "k2": peer_k2[l].astype(BF16),
            "u": u_all,
            "v_t": v_t_all,
        }
        ada = ada_all[l].reshape(COND_ROWS, 1, ADA_CHUNKS * d)
        xp, kv = _trunk_layer(xp, ada, 0, batch, seq, lw, None)
        for dst, arr in zip(new_kv, kv):
            heads = arr.shape[0]
            dst.append(arr.reshape(heads, batch, seq, HEAD_DIM).transpose(1, 2, 0, 3))
        xs, _ = _trunk_layer(xs, ada, 1, dec_batch, dec_seq, lw,
                             (cak[l], cav[l], cnk[l], cnv[l]))
    outs = [jnp.stack(lst, axis=1) for lst in new_kv]
    return (xp.reshape(batch, seq, d), xs.reshape(dec_batch, dec_seq, d), *outs)
```

```python
import functools
import math

import numpy as np
import jax
import jax.numpy as jnp
from jax import lax
from jax.experimental import pallas as pl
from jax.experimental.pallas import tpu as pltpu

D_MODEL = 1024
DEPTH = 2
GRID_W = 64
HEAD_DIM = 64
GROUP_WIDTH = 256
ATTN_HEADS = 4
ATTN_KV_HEADS = 2
NA_HEADS = 4
NA_KH = 8
NA_KW = 16
FOURIER_HEADS = 4
POOL_WINDOWS = (2, 4, 8, 16)
POOL_GROUP_W = GROUP_WIDTH // len(POOL_WINDOWS)
ROPE_THETA = 10000.0
ATTN_SCALE = HEAD_DIM ** -0.5
PEER_HEADS = 8
PEER_N_KEYS = 128
PEER_N_EXPERTS = PEER_N_KEYS * PEER_N_KEYS
PEER_KEY_DIM = 256
PEER_TOPK = 16
ALPHA = (2 * DEPTH) ** 0.25
LN_EPS = 1e-6
RMS_EPS = 1e-6
ADA_CHUNKS = 6
P_IN = 1792
COND_ROWS = 8

TOKEN_TILE = 512
ATTN_Q_TILE = 512
DFT_ROW_TILE = 512
NA_GROUP_ROWS = 4
PEER_TOKEN_TILE = 512
PEER_EXPERT_TILE = 2048
PEER_SUB_TILE = 256
PEER_LANE_CHUNK = 128
PEER_HEADS_PER_ITER = 2
ADA_COL_TILE = 2048
NEG_BIG = -1e30
VMEM_LIMIT = 56 * 1024 * 1024

F32 = jnp.float32
BF16 = jnp.bfloat16


def _params(*sem):
    return pltpu.CompilerParams(dimension_semantics=sem, vmem_limit_bytes=VMEM_LIMIT)


def _dot(a, b):
    return jnp.dot(a.astype(BF16), b.astype(BF16), preferred_element_type=F32)


def _dot_nt(a, b):
    return lax.dot_general(a.astype(BF16), b.astype(BF16), (((1,), (1,)), ((), ())),
                           preferred_element_type=F32)


def _layer_norm(x):
    mu = jnp.mean(x, axis=-1, keepdims=True)
    xc = x - mu
    var = jnp.mean(xc * xc, axis=-1, keepdims=True)
    return xc * lax.rsqrt(var + LN_EPS)


def _ada_kernel(cond_ref, w_ref, b_ref, o_ref):
    c = cond_ref[...]
    s = c / (1.0 + jnp.exp(-c))
    o_ref[...] = jnp.dot(s, w_ref[...], precision=lax.Precision.HIGHEST,
                         preferred_element_type=F32) + b_ref[...]


def _ada_call(cond, w_ada, b_ada):
    n = ADA_CHUNKS * D_MODEL
    return pl.pallas_call(
        _ada_kernel,
        out_shape=jax.ShapeDtypeStruct((DEPTH, COND_ROWS, n), F32),
        grid=(DEPTH, n // ADA_COL_TILE),
        in_specs=[
            pl.BlockSpec((COND_ROWS, D_MODEL), lambda l, j: (0, 0)),
            pl.BlockSpec((None, D_MODEL, ADA_COL_TILE), lambda l, j: (l, 0, j)),
            pl.BlockSpec((None, 1, ADA_COL_TILE), lambda l, j: (l, 0, j)),
        ],
        out_specs=pl.BlockSpec((None, COND_ROWS, ADA_COL_TILE), lambda l, j: (l, 0, j)),
        compiler_params=_params("parallel", "arbitrary"),
        name="ada_proj",
    )(cond, w_ada, b_ada.reshape(DEPTH, 1, n))


def _head_mean_sq(x, g):
    sq = x * x
    hi = sq.astype(BF16)
    lo = (sq - hi.astype(F32)).astype(BF16)
    return (jnp.dot(hi, g, preferred_element_type=F32)
            + jnp.dot(lo, g, preferred_element_type=F32))


def _rotary(x, cos, sin_signed):
    w = x.shape[-1]
    fwd = pltpu.roll(x, w - 16, axis=1)
    bwd = pltpu.roll(x, 16, axis=1)
    lane = lax.broadcasted_iota(jnp.int32, x.shape, 1)
    partner = jnp.where((lane % 32) < 16, fwd, bwd)
    return x * cos + partner * sin_signed


def _inproj_kernel(x_ref, ada_ref, w_ref, gq_ref, gk_ref, hm_ref, cos_ref, sin_ref,
                   aq_ref, ak_ref, av_ref, fx_ref, nq_ref, nk_ref, nv_ref, px_ref, *, rope):
    x = x_ref[...]
    sh = ada_ref[:, 0:D_MODEL]
    sc = ada_ref[:, D_MODEL:2 * D_MODEL]
    h = _layer_norm(x) * (1.0 + sc) + sh
    proj = _dot(h, w_ref[...])
    aq = proj[:, 0:256]
    ak = proj[:, 256:384]
    hm = hm_ref[...]
    aq = aq * lax.rsqrt(_head_mean_sq(aq, hm) + RMS_EPS) * gq_ref[...]
    ak = ak * lax.rsqrt(_head_mean_sq(ak, hm[0:128, 0:128]) + RMS_EPS) * gk_ref[...]
    if rope:
        cos = cos_ref[...]
        sin = sin_ref[...]
        aq = _rotary(aq, cos, sin)
        ak = _rotary(ak, cos[:, 0:128], sin[:, 0:128])
    for hd in range(ATTN_HEADS):
        aq_ref[hd] = aq[:, hd * 64:(hd + 1) * 64]
    for hd in range(ATTN_KV_HEADS):
        ak_ref[hd] = ak[:, hd * 64:(hd + 1) * 64]
        av_ref[hd] = proj[:, 384 + hd * 64:384 + (hd + 1) * 64]
    fx_ref[...] = proj[:, 512:768]
    for hd in range(NA_HEADS):
        nq_ref[hd] = proj[:, 768 + hd * 64:768 + (hd + 1) * 64]
        nk_ref[hd] = proj[:, 1024 + hd * 64:1024 + (hd + 1) * 64]
        nv_ref[hd] = proj[:, 1280 + hd * 64:1280 + (hd + 1) * 64]
    px_ref[...] = proj[:, 1536:1792]


def _inproj_call(x, ada, cond_base, cond_span, seq_len, w_in, gq, gk, head_mean, cos_t, sin_t,
                 rope):
    t = x.shape[0]
    tm = TOKEN_TILE
    tiles_per_seq = seq_len // tm if rope else 1
    tiles_per_cond = cond_span // tm
    hm = lambda n: jax.ShapeDtypeStruct((n, t, HEAD_DIM), F32)
    hspec = lambda n: pl.BlockSpec((n, tm, HEAD_DIM), lambda i: (0, i, 0))
    const = lambda shape: pl.BlockSpec(shape, lambda i: tuple(0 for _ in shape))
    return pl.pallas_call(
        functools.partial(_inproj_kernel, rope=rope),
        out_shape=(hm(4), hm(2), hm(2), jax.ShapeDtypeStruct((t, 256), F32),
                   hm(4), hm(4), hm(4), jax.ShapeDtypeStruct((t, 256), F32)),
        grid=(t // tm,),
        in_specs=[
            pl.BlockSpec((tm, D_MODEL), lambda i: (i, 0)),
            pl.BlockSpec((None, 1, ADA_CHUNKS * D_MODEL),
                         lambda i: (cond_base + i // tiles_per_cond, 0, 0)),
            const((D_MODEL, P_IN)),
            const((1, 256)), const((1, 128)), const((256, 256)),
            pl.BlockSpec((tm, 256), lambda i: (i % tiles_per_seq, 0)),
            pl.BlockSpec((tm, 256), lambda i: (i % tiles_per_seq, 0)),
        ],
        out_specs=(hspec(4), hspec(2), hspec(2), pl.BlockSpec((tm, 256), lambda i: (i, 0)),
                   hspec(4), hspec(4), hspec(4), pl.BlockSpec((tm, 256), lambda i: (i, 0))),
        compiler_params=_params("parallel"),
        name="inproj_rope" if rope else "inproj",
    )(x, ada, w_in, gq, gk, head_mean, cos_t, sin_t)


def _softmax_attend(q, k, v):
    s = _dot_nt(q * ATTN_SCALE, k)
    m = jnp.max(s, axis=-1, keepdims=True)
    p = jnp.exp(s - m)
    return _dot(p, v) / jnp.sum(p, axis=-1, keepdims=True)


def _ctx_attn_kernel(aq_ref, ak_ref, av_ref, nq_ref, nk_ref, nv_ref, a_ref, n_ref):
    outs = [_softmax_attend(aq_ref[hd], ak_ref[hd // 2], av_ref[hd // 2])
            for hd in range(ATTN_HEADS)]
    a_ref[...] = jnp.concatenate(outs, axis=-1)
    outs = [_softmax_attend(nq_ref[hd], nk_ref[hd], nv_ref[hd]) for hd in range(NA_HEADS)]
    n_ref[...] = jnp.concatenate(outs, axis=-1)


def _ctx_attn_call(aq, ak, av, nq, nk, nv, batch, seq_len):
    t = batch * seq_len
    hspec = lambda n: pl.BlockSpec((n, seq_len, HEAD_DIM), lambda b: (0, b, 0))
    ospec = pl.BlockSpec((seq_len, 256), lambda b: (b, 0))
    return pl.pallas_call(
        _ctx_attn_kernel,
        out_shape=(jax.ShapeDtypeStruct((t, 256), F32), jax.ShapeDtypeStruct((t, 256), F32)),
        grid=(batch,),
        in_specs=[hspec(4), hspec(2), hspec(2), hspec(4), hspec(4), hspec(4)],
        out_specs=(ospec, ospec),
        compiler_params=_params("parallel"),
        name="ctx_attention",
    )(aq, ak, av, nq, nk, nv)


def _lat_attn_kernel(q_ref, k_ref, v_ref, kc_ref, vc_ref, o_ref):
    k = k_ref[...]
    v = v_ref[...]
    kc = kc_ref[...]
    vc = vc_ref[...]
    outs = []
    for g in range(2):
        q = q_ref[g] * ATTN_SCALE
        s_l = _dot_nt(q, k)
        s_c = _dot_nt(q, kc)
        m = jnp.maximum(jnp.max(s_l, axis=-1, keepdims=True), jnp.max(s_c, axis=-1, keepdims=True))
        p_l = jnp.exp(s_l - m)
        p_c = jnp.exp(s_c - m)
        den = jnp.sum(p_l, axis=-1, keepdims=True) + jnp.sum(p_c, axis=-1, keepdims=True)
        outs.append((_dot(p_l, v) + _dot(p_c, vc)) / den)
    o_ref[...] = jnp.concatenate(outs, axis=-1)


def _lat_attn_call(aq, ak, av, kc, vc, batch, seq_len):
    t = batch * seq_len
    tq = ATTN_Q_TILE
    nqb = seq_len // tq
    return pl.pallas_call(
        _lat_attn_kernel,
        out_shape=jax.ShapeDtypeStruct((t, 256), F32),
        grid=(batch, ATTN_KV_HEADS, nqb),
        in_specs=[
            pl.BlockSpec((2, tq, HEAD_DIM), lambda b, g, i: (g, b * nqb + i, 0)),
            pl.BlockSpec((None, seq_len, HEAD_DIM), lambda b, g, i: (g, b, 0)),
            pl.BlockSpec((None, seq_len, HEAD_DIM), lambda b, g, i: (g, b, 0)),
            pl.BlockSpec((None, None, kc.shape[2], HEAD_DIM), lambda b, g, i: (b, g, 0, 0)),
            pl.BlockSpec((None, None, kc.shape[2], HEAD_DIM), lambda b, g, i: (b, g, 0, 0)),
        ],
        out_specs=pl.BlockSpec((tq, 128), lambda b, g, i: (b * nqb + i, g)),
        compiler_params=_params("parallel", "parallel", "arbitrary"),
        name="lat_attention",
    )(aq, ak, av, kc, vc)


def _na_geometry(rows):
    kh = min(NA_KH, rows)
    gr = NA_GROUP_ROWS
    kb = min(gr + kh, rows)
    starts, classes, tables = [], [], []
    for g in range(rows // gr):
        us = int(np.clip(gr * g - kh // 2, 0, rows - kb))
        rel = np.full((gr, kb), -1, np.int64)
        for dr in range(gr):
            r = gr * g + dr
            rs = int(np.clip(r - kh // 2, 0, rows - kh))
            assert us <= rs and rs + kh <= us + kb
            for k in range(kh):
                rel[dr, rs + k - us] = rs + k - r + (NA_KH - 1)
        for ci, tab in enumerate(tables):
            if np.array_equal(tab, rel):
                classes.append(ci)
                break
        else:
            classes.append(len(tables))
            tables.append(rel)
        starts.append(us)
    return kb, starts, classes, tables


def _na_kernel(start_ref, class_ref, q_ref, k_ref, v_ref, kc_ref, vc_ref, bias_ref, o_ref, *,
               rows, key_rows):
    gq = NA_GROUP_ROWS * GRID_W
    gk = key_rows * GRID_W

    def one_group(g, carry):
        q_off = pl.multiple_of(g * gq, gq)
        k_off = pl.multiple_of(start_ref[g] * GRID_W, GRID_W)
        cls = class_ref[g]
        outs = []
        for hd in range(2):
            q = q_ref[hd, pl.ds(q_off, gq), :] * ATTN_SCALE
            kw = k_ref[hd, pl.ds(k_off, gk), :]
            vw = v_ref[hd, pl.ds(k_off, gk), :]
            s_w = _dot_nt(q, kw) + bias_ref[hd, cls]
            s_c = _dot_nt(q, kc_ref[hd])
            m = jnp.maximum(jnp.max(s_w, axis=-1, keepdims=True),
                            jnp.max(s_c, axis=-1, keepdims=True))
            p_w = jnp.exp(s_w - m)
            p_c = jnp.exp(s_c - m)
            den = jnp.sum(p_w, axis=-1, keepdims=True) + jnp.sum(p_c, axis=-1, keepdims=True)
            outs.append((_dot(p_w, vw) + _dot(p_c, vc_ref[hd])) / den)
        o_ref[pl.ds(q_off, gq), :] = jnp.concatenate(outs, axis=-1)
        return carry

    lax.fori_loop(0, rows // NA_GROUP_ROWS, one_group, 0)


def _na_call(nq, nk, nv, kc, vc, bias, batch, seq_len):
    t = batch * seq_len
    rows = seq_len // GRID_W
    key_rows, starts, classes, _ = _na_geometry(rows)
    lspec = pl.BlockSpec((2, seq_len, HEAD_DIM), lambda b, hp, *_: (hp, b, 0))
    cspec = pl.BlockSpec((None, 2, kc.shape[2], HEAD_DIM), lambda b, hp, *_: (b, hp, 0, 0))
    return pl.pallas_call(
        functools.partial(_na_kernel, rows=rows, key_rows=key_rows),
        out_shape=jax.ShapeDtypeStruct((t, 256), F32),
        grid_spec=pltpu.PrefetchScalarGridSpec(
            num_scalar_prefetch=2,
            grid=(batch, NA_HEADS // 2),
            in_specs=[lspec, lspec, lspec, cspec, cspec,
                      pl.BlockSpec((2,) + bias.shape[1:], lambda b, hp, *_: (hp, 0, 0, 0))],
            out_specs=pl.BlockSpec((seq_len, 128), lambda b, hp, *_: (b, hp)),
        ),
        compiler_params=_params("parallel", "parallel"),
        name="neighbourhood_attention",
    )(jnp.asarray(starts, jnp.int32), jnp.asarray(classes, jnp.int32), nq, nk, nv, kc, vc, bias)


def _na_bias_patterns(bias_table, rows):
    key_rows, _, _, tables = _na_geometry(rows)
    rel = np.stack(tables)
    c = np.arange(GRID_W)
    cs = np.clip(c - NA_KW // 2, 0, GRID_W - NA_KW)
    kc = np.arange(GRID_W)
    inside = (kc[None, :] >= cs[:, None]) & (kc[None, :] < cs[:, None] + NA_KW)
    col_rel = np.clip(kc[None, :] - c[:, None] + (NA_KW - 1), 0, 2 * NA_KW - 2)
    col_pick = (col_rel[:, :, None] == np.arange(2 * NA_KW - 1)).astype(np.float32)
    blocks = jnp.einsum("hrs,cqs->hrcq", bias_table, col_pick, precision=lax.Precision.HIGHEST)
    blocks = jnp.where(inside[None, None], blocks, NEG_BIG)
    masked = jnp.full(blocks[:, 0].shape, NEG_BIG, F32)
    per_class = []
    for tab in rel:
        per_row = [jnp.concatenate([blocks[:, ro] if ro >= 0 else masked for ro in row], axis=-1)
                   for row in tab]
        per_class.append(jnp.concatenate(per_row, axis=1))
    return jnp.stack(per_class, axis=1)


def _fourier_kernel(x_ref, cl_ref, sl_ref, cc_ref, sc_ref, w_ref, o_ref, *, norm):
    x = x_ref[...]
    xc = _dot(x, cc_ref[...])
    xs = _dot(x, sc_ref[...])
    y = (_dot(cl_ref[...], xc) - _dot(sl_ref[...], xs)) * norm
    o_ref[...] = _dot(y, w_ref[...])


def _fourier_call(fx, w_fourier, batch, seq_len):
    t = batch * seq_len
    tm = min(DFT_ROW_TILE, seq_len)
    nmb = seq_len // tm
    j = np.arange(seq_len, dtype=np.float64)
    ang_l = 2.0 * np.pi * ((j[:, None] * j[None, :]) % seq_len) / seq_len
    hw = GROUP_WIDTH // FOURIER_HEADS
    c = np.arange(hw, dtype=np.float64)
    ang_c = 2.0 * np.pi * ((c[:, None] * c[None, :]) % hw) / hw
    eye = np.eye(FOURIER_HEADS)
    cl = jnp.asarray(np.cos(ang_l), F32)
    sl = jnp.asarray(np.sin(ang_l), F32)
    cc = jnp.asarray(np.kron(eye, np.cos(ang_c)), F32)
    sc = jnp.asarray(np.kron(eye, np.sin(ang_c)), F32)
    const = lambda shape: pl.BlockSpec(shape, lambda i, b: (0, 0))
    return pl.pallas_call(
        functools.partial(_fourier_kernel, norm=float((seq_len * hw) ** -0.5)),
        out_shape=jax.ShapeDtypeStruct((t, 256), F32),
        grid=(nmb, batch),
        in_specs=[
            pl.BlockSpec((seq_len, 256), lambda i, b: (b, 0)),
            pl.BlockSpec((tm, seq_len), lambda i, b: (i, 0)),
            pl.BlockSpec((tm, seq_len), lambda i, b: (i, 0)),
            const((256, 256)), const((256, 256)), const((256, 256)),
        ],
        out_specs=pl.BlockSpec((tm, 256), lambda i, b: (b * nmb + i, 0)),
        compiler_params=_params("parallel", "arbitrary"),
        name="fourier_mix",
    )(fx, cl, sl, cc, sc, w_fourier)


POOL_PAD = 8


def _pool_kernel(x_ref, w_ref, s_ref, o_ref, pad_ref, *, seq_len):
    x = x_ref[...]
    zeros = jnp.zeros((POOL_PAD, GROUP_WIDTH), F32)
    pad_ref[0:POOL_PAD, :] = zeros
    pad_ref[POOL_PAD + seq_len:2 * POOL_PAD + seq_len, :] = zeros
    pad_ref[POOL_PAD:POOL_PAD + seq_len, :] = x

    def shifted(off):
        return pad_ref[POOL_PAD + off:POOL_PAD + off + seq_len, :]

    t = lax.broadcasted_iota(jnp.int32, (seq_len, GROUP_WIDTH), 0)
    lane_group = lax.broadcasted_iota(jnp.int32, (seq_len, GROUP_WIDTH), 1) // POOL_GROUP_W
    acc = x
    done = 0
    pooled = jnp.zeros_like(x)
    for g, w in enumerate(POOL_WINDOWS):
        half = w // 2
        for off in list(range(-half, -done)) + list(range(max(done, 1), half)):
            acc = acc + shifted(off)
        done = half
        cnt = jnp.minimum(t + half, seq_len) - jnp.maximum(t - half, 0)
        pooled = jnp.where(lane_group == g, acc / cnt.astype(F32), pooled)
    o_ref[...] = _dot(pooled - x, w_ref[...]) * s_ref[...]


def _pool_call(px, w_pool_bd, pool_scale, batch, seq_len):
    t = batch * seq_len
    return pl.pallas_call(
        functools.partial(_pool_kernel, seq_len=seq_len),
        out_shape=jax.ShapeDtypeStruct((t, 256), F32),
        grid=(batch,),
        in_specs=[
            pl.BlockSpec((seq_len, 256), lambda b: (b, 0)),
            pl.BlockSpec((256, 256), lambda b: (0, 0)),
            pl.BlockSpec((1, 256), lambda b: (0, 0)),
        ],
        out_specs=pl.BlockSpec((seq_len, 256), lambda b: (b, 0)),
        scratch_shapes=[pltpu.VMEM((seq_len + 2 * POOL_PAD, GROUP_WIDTH), F32)],
        compiler_params=_params("parallel"),
        name="multiscale_pool",
    )(px, w_pool_bd, pool_scale)


def _outproj_kernel(a_ref, f_ref, n_ref, p_ref, x_ref, ada_ref, w_ref, g_ref, b_ref, o_ref):
    mix = (_dot(a_ref[...], w_ref[0:256, :]) + _dot(f_ref[...], w_ref[256:512, :])
           + _dot(n_ref[...], w_ref[512:768, :]) + _dot(p_ref[...], w_ref[768:1024, :]))
    gate = ada_ref[:, 2 * D_MODEL:3 * D_MODEL]
    z = ALPHA * x_ref[...] + gate * mix
    o_ref[...] = _layer_norm(z) * g_ref[...] + b_ref[...]


def _outproj_call(a_out, f_out, n_out, p_out, x, ada, cond_base, cond_span, w_out, ln_g, ln_b):
    t = x.shape[0]
    tm = TOKEN_TILE
    tiles_per_cond = cond_span // tm
    part = pl.BlockSpec((tm, 256), lambda i: (i, 0))
    row = pl.BlockSpec((1, D_MODEL), lambda i: (0, 0))
    return pl.pallas_call(
        _outproj_kernel,
        out_shape=jax.ShapeDtypeStruct((t, D_MODEL), F32),
        grid=(t // tm,),
        in_specs=[part, part, part, part,
                  pl.BlockSpec((tm, D_MODEL), lambda i: (i, 0)),
                  pl.BlockSpec((None, 1, ADA_CHUNKS * D_MODEL),
                               lambda i: (cond_base + i // tiles_per_cond, 0, 0)),
                  pl.BlockSpec((D_MODEL, D_MODEL), lambda i: (0, 0)), row, row],
        out_specs=pl.BlockSpec((tm, D_MODEL), lambda i: (i, 0)),
        compiler_params=_params("parallel"),
        name="outproj_postnorm",
    )(a_out, f_out, n_out, p_out, x, ada, w_out, ln_g, ln_b)


def _sorting_network(n):
    pairs = []
    p = 1
    while p < n:
        k = p
        while k >= 1:
            for j in range(k % p, n - k, 2 * k):
                for i in range(min(k, n - j - k)):
                    if (i + j) // (2 * p) == (i + j + k) // (2 * p):
                        pairs.append((i + j, i + j + k))
            k //= 2
        p *= 2
    return pairs


def _sort_descending(rows):
    rows = list(rows)
    for a, b in _sorting_network(len(rows)):
        rows[a], rows[b] = jnp.maximum(rows[a], rows[b]), jnp.minimum(rows[a], rows[b])
    return rows


def _sublane_max_all(x):
    x = jnp.maximum(x, pltpu.roll(x, 4, axis=0))
    x = jnp.maximum(x, pltpu.roll(x, 2, axis=0))
    return jnp.maximum(x, pltpu.roll(x, 1, axis=0))


def _extract_top(groups, count):
    groups = [list(g) for g in groups]
    out = []
    for it in range(count):
        head = groups[0][0]
        for g in groups[1:]:
            head = jnp.maximum(head, g[0])
        m = _sublane_max_all(head)
        out.append(m)
        remaining = count - it - 1
        if remaining == 0:
            break
        for g in groups:
            eq = g[0] == m
            keep = min(len(g), remaining)
            for i in range(keep):
                nxt = g[i + 1] if i + 1 < len(g) else jnp.full_like(m, -jnp.inf)
                g[i] = jnp.where(eq, nxt, g[i])
            del g[keep:]
    return out


def _peer_route(s1, s2):
    groups = PEER_N_KEYS // 8
    s1_rows = [s1[8 * i:8 * (i + 1)] for i in range(groups)]
    s2_rows = [s2[8 * i:8 * (i + 1)] for i in range(groups)]
    v1 = _extract_top([_sort_descending(s1_rows)], PEER_TOPK + 1)
    v2 = _extract_top([_sort_descending(s2_rows)], PEER_TOPK + 1)
    sub = lax.broadcasted_iota(jnp.int32, v1[0].shape, 0)
    lo = v1[0]
    hi = v1[8]
    for r in range(1, 8):
        lo = jnp.where(sub == r, v1[r], lo)
        hi = jnp.where(sub == r, v1[8 + r], hi)
    cand = [[lo + v2[b] for b in range(PEER_TOPK + 1)], [hi + v2[0]], [v1[PEER_TOPK] + v2[0]]]
    top = _extract_top(cand, PEER_TOPK + 1)
    den = jnp.ones_like(top[0])
    for tk in top[1:PEER_TOPK]:
        den = den + jnp.exp(tk - top[0])
    half_inv_den = 0.5 / den
    tau = 0.5 * (top[PEER_TOPK - 1] + top[PEER_TOPK])
    thr = [tau - r for r in s1_rows]
    coef = [jnp.exp(r - v1[0]) * half_inv_den for r in s1_rows]
    e2 = [jnp.exp(r - v2[0]) for r in s2_rows]
    return thr, coef, e2


def _peer_kernel(x_ref, ada_ref, wq_ref, k1_ref, k2_ref, u_ref, vt_ref, g_ref, b_ref, o_ref,
                 h_ref, ht_ref, yt_ref, act_ref, s2_ref, e2_ref, thr_ref, coef_ref):
    j = pl.program_id(1)
    tt = x_ref.shape[0]
    half = PEER_KEY_DIM // 2
    lanes = PEER_LANE_CHUNK

    @pl.when(j == 0)
    def _():
        sh = ada_ref[:, 3 * D_MODEL:4 * D_MODEL]
        sc = ada_ref[:, 4 * D_MODEL:5 * D_MODEL]
        h = _layer_norm(x_ref[...]) * (1.0 + sc) + sh
        h_ref[...] = h.astype(BF16)
        ht_ref[...] = h.T.astype(BF16)
        yt_ref[...] = jnp.zeros_like(yt_ref)

        def one_head(hd):
            w_rows = wq_ref[pl.ds(pl.multiple_of(hd * PEER_KEY_DIM, PEER_KEY_DIM), PEER_KEY_DIM), :]
            q_t = lax.dot_general(w_rows, h_ref[...], (((1,), (1,)), ((), ())),
                                  preferred_element_type=F32)
            s1 = _dot(k1_ref[...], q_t[0:half])
            s2 = _dot(k2_ref[...], q_t[half:2 * half])
            s2_ref[hd] = s2
            for c in range(tt // lanes):
                cols = slice(c * lanes, (c + 1) * lanes)
                thr, coef, e2 = _peer_route(s1[:, cols], s2[:, cols])
                for i in range(PEER_N_KEYS // 8):
                    rows = pl.ds(8 * i, 8)
                    thr_ref[hd, rows, cols] = thr[i]
                    coef_ref[hd, rows, cols] = coef[i]
                    e2_ref[hd, rows, cols] = e2[i]

        def head_group(g, carry):
            for k in range(PEER_HEADS_PER_ITER):
                one_head(g * PEER_HEADS_PER_ITER + k)
            return carry

        lax.fori_loop(0, PEER_HEADS // PEER_HEADS_PER_ITER, head_group, 0)

    keys_per_block = PEER_EXPERT_TILE // PEER_N_KEYS
    keys_per_sub = PEER_SUB_TILE // PEER_N_KEYS
    n_sub = PEER_EXPERT_TILE // PEER_SUB_TILE

    def up_proj(sub):
        return jnp.dot(u_ref[sub * PEER_SUB_TILE:(sub + 1) * PEER_SUB_TILE, :], ht_ref[...],
                       preferred_element_type=F32)

    acc = None
    a_next = up_proj(0)
    for sub in range(n_sub):
        sub_rows = slice(sub * PEER_SUB_TILE, (sub + 1) * PEER_SUB_TILE)
        a_sub = a_next
        if sub + 1 < n_sub:
            a_next = up_proj(sub + 1)
        for ii in range(keys_per_sub):
            i1 = j * keys_per_block + sub * keys_per_sub + ii
            rows = slice(ii * PEER_N_KEYS, (ii + 1) * PEER_N_KEYS)
            thr_rows = [thr_ref[hd, pl.ds(i1, 1), :] for hd in range(PEER_HEADS)]
            coef_rows = [coef_ref[hd, pl.ds(i1, 1), :] for hd in range(PEER_HEADS)]
            for c in range(tt // lanes):
                cols = slice(c * lanes, (c + 1) * lanes)
                w = None
                for hd in range(PEER_HEADS):
                    thr = thr_rows[hd][:, cols]
                    coef = coef_rows[hd][:, cols]
                    term = jnp.where(s2_ref[hd, :, cols] >= thr, e2_ref[hd, :, cols], 0.0) * coef
                    w = term if w is None else w + term
                a = a_sub[rows, cols]
                gelu2 = a * (1.0 + lax.erf(a * (2.0 ** -0.5)))
                act_ref[sub * PEER_SUB_TILE + ii * PEER_N_KEYS:
                        sub * PEER_SUB_TILE + (ii + 1) * PEER_N_KEYS, cols] = (w * gelu2).astype(BF16)
        part = jnp.dot(vt_ref[:, sub_rows], act_ref[sub_rows, :], preferred_element_type=F32)
        acc = part if acc is None else acc + part
    yt_ref[...] += acc

    @pl.when(j == pl.num_programs(1) - 1)
    def _():
        gate = ada_ref[:, 5 * D_MODEL:6 * D_MODEL]
        z = ALPHA * x_ref[...] + gate * yt_ref[...].T
        o_ref[...] = _layer_norm(z) * g_ref[...] + b_ref[...]


def _peer_call(x, ada, cond_base, cond_span, layer, wq_t, k1, k2, u, v_t, ln_g, ln_b):
    t = x.shape[0]
    tt = PEER_TOKEN_TILE
    tiles_per_cond = cond_span // tt
    eb = PEER_EXPERT_TILE
    row = pl.BlockSpec((1, D_MODEL), lambda i, j: (0, 0))
    per_head = pltpu.VMEM((PEER_HEADS, PEER_N_KEYS, tt), F32)
    return pl.pallas_call(
        _peer_kernel,
        out_shape=jax.ShapeDtypeStruct((t, D_MODEL), F32),
        grid=(t // tt, PEER_N_EXPERTS // eb),
        in_specs=[
            pl.BlockSpec((tt, D_MODEL), lambda i, j: (i, 0)),
            pl.BlockSpec((None, 1, ADA_CHUNKS * D_MODEL),
                         lambda i, j: (cond_base + i // tiles_per_cond, 0, 0)),
            pl.BlockSpec((PEER_HEADS * PEER_KEY_DIM, D_MODEL), lambda i, j: (0, 0)),
            pl.BlockSpec((PEER_N_KEYS, PEER_KEY_DIM // 2), lambda i, j: (0, 0)),
            pl.BlockSpec((PEER_N_KEYS, PEER_KEY_DIM // 2), lambda i, j: (0, 0)),
            pl.BlockSpec((None, eb, D_MODEL), lambda i, j: (layer, j, 0)),
            pl.BlockSpec((None, D_MODEL, eb), lambda i, j: (layer, 0, j)),
            row, row,
        ],
        out_specs=pl.BlockSpec((tt, D_MODEL), lambda i, j: (i, 0)),
        scratch_shapes=[
            pltpu.VMEM((tt, D_MODEL), BF16),
            pltpu.VMEM((D_MODEL, tt), BF16),
            pltpu.VMEM((D_MODEL, tt), F32),
            pltpu.VMEM((eb, tt), BF16),
            per_head, per_head, per_head, per_head,
        ],
        compiler_params=_params("parallel", "arbitrary"),
        name="peer",
    )(x, ada, wq_t, k1, k2, u, v_t, ln_g, ln_b)


def _rope_tables(seq_len):
    t = np.arange(seq_len)
    row = (t // GRID_W).astype(np.float64)
    col = (t % GRID_W).astype(np.float64)
    nf = HEAD_DIM // 4
    inv = ROPE_THETA ** (-np.arange(nf, dtype=np.float64) / nf)
    d = np.arange(HEAD_DIM)
    pos = np.where(d[None, :] < HEAD_DIM // 2, row[:, None], col[:, None])
    ang = pos * inv[d % nf][None, :]
    sign = np.where((d % (HEAD_DIM // 2)) < nf, -1.0, 1.0)
    cos = np.tile(np.cos(ang), (1, ATTN_HEADS))
    sin = np.tile(np.sin(ang) * sign[None, :], (1, ATTN_HEADS))
    return jnp.asarray(cos, F32), jnp.asarray(sin, F32)


def _head_mean_matrix():
    m = np.kron(np.eye(ATTN_HEADS), np.full((HEAD_DIM, HEAD_DIM), 1.0 / HEAD_DIM))
    return jnp.asarray(m, BF16)


def _block_diag(w):
    g, a, b = w.shape
    out = jnp.zeros((g * a, g * b), w.dtype)
    for i in range(g):
        out = out.at[i * a:(i + 1) * a, i * b:(i + 1) * b].set(w[i])
    return out


def _trunk_layer(x, ada, cond_base, batch, seq_len, lw, ctx_cache):
    rope = ctx_cache is not None
    if rope:
        cos_t, sin_t = _rope_tables(seq_len)
    else:
        cos_t = jnp.ones((TOKEN_TILE, 256), F32)
        sin_t = jnp.zeros((TOKEN_TILE, 256), F32)
    cond_span = seq_len if rope else batch * seq_len
    aq, ak, av, fx, nq, nk, nv, px = _inproj_call(
        x, ada, cond_base, cond_span, seq_len, lw["w_in"], lw["gq"], lw["gk"], lw["head_mean"],
        cos_t, sin_t, rope)
    if ctx_cache is None:
        a_out, n_out = _ctx_attn_call(aq, ak, av, nq, nk, nv, batch, seq_len)
    else:
        ck_a, cv_a, ck_n, cv_n = ctx_cache
        a_out = _lat_attn_call(aq, ak, av, ck_a, cv_a, batch, seq_len)
        n_out = _na_call(nq, nk, nv, ck_n, cv_n, lw["na_bias"], batch, seq_len)
    f_out = _fourier_call(fx, lw["w_fourier"], batch, seq_len)
    p_out = _pool_call(px, lw["w_pool"], lw["pool_scale"], batch, seq_len)
    x = _outproj_call(a_out, f_out, n_out, p_out, x, ada, cond_base, cond_span,
                      lw["w_out"], lw["ln1_g"], lw["ln1_b"])
    x = _peer_call(x, ada, cond_base, cond_span, lw["layer"], lw["wq_t"], lw["k1"], lw["k2"],
                   lw["u"], lw["v_t"], lw["ln2_g"], lw["ln2_b"])
    return x, (ak, av, nk, nv)


def kernel(x_prompt, x_sample, cache_attn_k, cache_attn_v, cache_na_k, cache_na_v, c, c_ctx,
           w_ada, b_ada, w_in, q_norm_g, k_norm_g, w_fourier, na_bias, w_pool, pool_scale,
           w_out, ln1_g, ln1_b, ln2_g, ln2_b, peer_wq, peer_k1, peer_k2, peer_u, peer_v):
    batch, seq, d = x_prompt.shape
    dec_batch, dec_seq, _ = x_sample.shape
    xp = x_prompt.reshape(batch * seq, d)
    xs = x_sample.reshape(dec_batch * dec_seq, d)

    cond = jnp.zeros((COND_ROWS, d), F32).at[0].set(c_ctx).at[1:1 + dec_batch].set(c)
    ada_all = _ada_call(cond, w_ada, b_ada)
    head_mean = _head_mean_matrix()
    rows = dec_seq // GRID_W

    def heads_first(cache):
        return cache.transpose(1, 0, 3, 2, 4)

    cak, cav = heads_first(cache_attn_k), heads_first(cache_attn_v)
    cnk, cnv = heads_first(cache_na_k), heads_first(cache_na_v)

    u_all = peer_u.astype(BF16)
    v_t_all = jnp.swapaxes(peer_v.astype(BF16), 1, 2)

    new_kv = [[], [], [], []]
    for l in range(DEPTH):
        lw = {
            "layer": l,
            "w_in": w_in[l].astype(BF16),
            "gq": jnp.tile(q_norm_g[l], ATTN_HEADS).reshape(1, -1),
            "gk": jnp.tile(k_norm_g[l], ATTN_KV_HEADS).reshape(1, -1),
            "head_mean": head_mean,
            "w_fourier": w_fourier[l].astype(BF16),
            "na_bias": _na_bias_patterns(na_bias[l], rows),
            "w_pool": _block_diag(w_pool[l]).astype(BF16),
            "pool_scale": pool_scale[l].reshape(1, -1),
            "w_out": w_out[l].astype(BF16),
            "ln1_g": ln1_g[l].reshape(1, -1), "ln1_b": ln1_b[l].reshape(1, -1),
            "ln2_g": ln2_g[l].reshape(1, -1), "ln2_b": ln2_b[l].reshape(1, -1),
            "wq_t": peer_wq[l].T.astype(BF16),
            "k1": peer_k1[l].astype(BF16), "k2": peer_k2[l].astype(BF16),
            "u": u_all,
            "v_t": v_t_all,
        }
        ada = ada_all[l].reshape(COND_ROWS, 1, ADA_CHUNKS * d)
        xp, kv = _trunk_layer(xp, ada, 0, batch, seq, lw, None)
        for dst, arr in zip(new_kv, kv):
            heads = arr.shape[0]
            dst.append(arr.reshape(heads, batch, seq, HEAD_DIM).transpose(1, 2, 0, 3))
        xs, _ = _trunk_layer(xs, ada, 1, dec_batch, dec_seq, lw,
                             (cak[l], cav[l], cnk[l], cnv[l]))
    outs = [jnp.stack(lst, axis=1) for lst in new_kv]
    return (xp.reshape(batch, seq, d), xs.reshape(dec_batch, dec_seq, d), *outs)
```

```python
import functools
import math

import numpy as np
import jax
import jax.numpy as jnp
from jax import lax
from jax.experimental import pallas as pl
from jax.experimental.pallas import tpu as pltpu

D_MODEL = 1024
DEPTH = 2
GRID_W = 64
HEAD_DIM = 64
GROUP_WIDTH = 256
ATTN_HEADS = 4
ATTN_KV_HEADS = 2
NA_HEADS = 4
NA_KH = 8
NA_KW = 16
FOURIER_HEADS = 4
POOL_WINDOWS = (2, 4, 8, 16)
POOL_GROUP_W = GROUP_WIDTH // len(POOL_WINDOWS)
ROPE_THETA = 10000.0
ATTN_SCALE = HEAD_DIM ** -0.5
PEER_HEADS = 8
PEER_N_KEYS = 128
PEER_N_EXPERTS = PEER_N_KEYS * PEER_N_KEYS
PEER_KEY_DIM = 256
PEER_TOPK = 16
ALPHA = (2 * DEPTH) ** 0.25
LN_EPS = 1e-6
RMS_EPS = 1e-6
ADA_CHUNKS = 6
P_IN = 1792
COND_ROWS = 8

TOKEN_TILE = 512
ATTN_Q_TILE = 512
DFT_ROW_TILE = 512
NA_GROUP_ROWS = 4
PEER_TOKEN_TILE = 512
PEER_EXPERT_TILE = 2048
PEER_SUB_TILE = 256
PEER_LANE_CHUNK = 128
PEER_HEADS_PER_ITER = 4
ADA_COL_TILE = 2048
NEG_BIG = -1e30
VMEM_LIMIT = 56 * 1024 * 1024

F32 = jnp.float32
BF16 = jnp.bfloat16


def _params(*sem):
    return pltpu.CompilerParams(dimension_semantics=sem, vmem_limit_bytes=VMEM_LIMIT)


def _dot(a, b):
    return jnp.dot(a.astype(BF16), b.astype(BF16), preferred_element_type=F32)


def _dot_nt(a, b):
    return lax.dot_general(a.astype(BF16), b.astype(BF16), (((1,), (1,)), ((), ())),
                           preferred_element_type=F32)


def _layer_norm(x):
    mu = jnp.mean(x, axis=-1, keepdims=True)
    xc = x - mu
    var = jnp.mean(xc * xc, axis=-1, keepdims=True)
    return xc * lax.rsqrt(var + LN_EPS)


def _ada_kernel(cond_ref, w_ref, b_ref, o_ref):
    c = cond_ref[...]
    s = c / (1.0 + jnp.exp(-c))
    o_ref[...] = jnp.dot(s, w_ref[...], precision=lax.Precision.HIGHEST,
                         preferred_element_type=F32) + b_ref[...]


def _ada_call(cond, w_ada, b_ada):
    n = ADA_CHUNKS * D_MODEL
    return pl.pallas_call(
        _ada_kernel,
        out_shape=jax.ShapeDtypeStruct((DEPTH, COND_ROWS, n), F32),
        grid=(DEPTH, n // ADA_COL_TILE),
        in_specs=[
            pl.BlockSpec((COND_ROWS, D_MODEL), lambda l, j: (0, 0)),
            pl.BlockSpec((None, D_MODEL, ADA_COL_TILE), lambda l, j: (l, 0, j)),
            pl.BlockSpec((None, 1, ADA_COL_TILE), lambda l, j: (l, 0, j)),
        ],
        out_specs=pl.BlockSpec((None, COND_ROWS, ADA_COL_TILE), lambda l, j: (l, 0, j)),
        compiler_params=_params("parallel", "arbitrary"),
        name="ada_proj",
    )(cond, w_ada, b_ada.reshape(DEPTH, 1, n))


def _head_mean_sq(x, g):
    sq = x * x
    hi = sq.astype(BF16)
    lo = (sq - hi.astype(F32)).astype(BF16)
    return (jnp.dot(hi, g, preferred_element_type=F32)
            + jnp.dot(lo, g, preferred_element_type=F32))


def _rotary(x, cos, sin_signed):
    w = x.shape[-1]
    fwd = pltpu.roll(x, w - 16, axis=1)
    bwd = pltpu.roll(x, 16, axis=1)
    lane = lax.broadcasted_iota(jnp.int32, x.shape, 1)
    partner = jnp.where((lane % 32) < 16, fwd, bwd)
    return x * cos + partner * sin_signed


def _inproj_kernel(x_ref, ada_ref, w_ref, gq_ref, gk_ref, hm_ref, cos_ref, sin_ref,
                   aq_ref, ak_ref, av_ref, fx_ref, nq_ref, nk_ref, nv_ref, px_ref, *, rope):
    x = x_ref[...]
    sh = ada_ref[:, 0:D_MODEL]
    sc = ada_ref[:, D_MODEL:2 * D_MODEL]
    h = _layer_norm(x) * (1.0 + sc) + sh
    proj = _dot(h, w_ref[...])
    aq = proj[:, 0:256]
    ak = proj[:, 256:384]
    hm = hm_ref[...]
    aq = aq * lax.rsqrt(_head_mean_sq(aq, hm) + RMS_EPS) * gq_ref[...]
    ak = ak * lax.rsqrt(_head_mean_sq(ak, hm[0:128, 0:128]) + RMS_EPS) * gk_ref[...]
    if rope:
        cos = cos_ref[...]
        sin = sin_ref[...]
        aq = _rotary(aq, cos, sin)
        ak = _rotary(ak, cos[:, 0:128], sin[:, 0:128])
    for hd in range(ATTN_HEADS):
        aq_ref[hd] = aq[:, hd * 64:(hd + 1) * 64]
    for hd in range(ATTN_KV_HEADS):
        ak_ref[hd] = ak[:, hd * 64:(hd + 1) * 64]
        av_ref[hd] = proj[:, 384 + hd * 64:384 + (hd + 1) * 64]
    fx_ref[...] = proj[:, 512:768]
    for hd in range(NA_HEADS):
        nq_ref[hd] = proj[:, 768 + hd * 64:768 + (hd + 1) * 64]
        nk_ref[hd] = proj[:, 1024 + hd * 64:1024 + (hd + 1) * 64]
        nv_ref[hd] = proj[:, 1280 + hd * 64:1280 + (hd + 1) * 64]
    px_ref[...] = proj[:, 1536:1792]


def _inproj_call(x, ada, cond_base, cond_span, seq_len, w_in, gq, gk, head_mean, cos_t, sin_t,
                 rope):
    t = x.shape[0]
    tm = TOKEN_TILE
    tiles_per_seq = seq_len // tm if rope else 1
    tiles_per_cond = cond_span // tm
    hm = lambda n: jax.ShapeDtypeStruct((n, t, HEAD_DIM), F32)
    hspec = lambda n: pl.BlockSpec((n, tm, HEAD_DIM), lambda i: (0, i, 0))
    const = lambda shape: pl.BlockSpec(shape, lambda i: tuple(0 for _ in shape))
    return pl.pallas_call(
        functools.partial(_inproj_kernel, rope=rope),
        out_shape=(hm(4), hm(2), hm(2), jax.ShapeDtypeStruct((t, 256), F32),
                   hm(4), hm(4), hm(4), jax.ShapeDtypeStruct((t, 256), F32)),
        grid=(t // tm,),
        in_specs=[
            pl.BlockSpec((tm, D_MODEL), lambda i: (i, 0)),
            pl.BlockSpec((None, 1, ADA_CHUNKS * D_MODEL),
                         lambda i: (cond_base + i // tiles_per_cond, 0, 0)),
            const((D_MODEL, P_IN)),
            const((1, 256)), const((1, 128)), const((256, 256)),
            pl.BlockSpec((tm, 256), lambda i: (i % tiles_per_seq, 0)),
            pl.BlockSpec((tm, 256), lambda i: (i % tiles_per_seq, 0)),
        ],
        out_specs=(hspec(4), hspec(2), hspec(2), pl.BlockSpec((tm, 256), lambda i: (i, 0)),
                   hspec(4), hspec(4), hspec(4), pl.BlockSpec((tm, 256), lambda i: (i, 0))),
        compiler_params=_params("parallel"),
        name="inproj_rope" if rope else "inproj",
    )(x, ada, w_in, gq, gk, head_mean, cos_t, sin_t)


def _softmax_attend(q, k, v):
    s = _dot_nt(q * ATTN_SCALE, k)
    m = jnp.max(s, axis=-1, keepdims=True)
    p = jnp.exp(s - m)
    return _dot(p, v) / jnp.sum(p, axis=-1, keepdims=True)


def _ctx_attn_kernel(aq_ref, ak_ref, av_ref, nq_ref, nk_ref, nv_ref, a_ref, n_ref):
    seq = aq_ref.shape[1]
    outs = []
    for g in range(ATTN_KV_HEADS):
        q2 = jnp.concatenate([aq_ref[2 * g], aq_ref[2 * g + 1]], axis=0)
        o2 = _softmax_attend(q2, ak_ref[g], av_ref[g])
        outs += [o2[0:seq], o2[seq:2 * seq]]
    a_ref[...] = jnp.concatenate(outs, axis=-1)
    outs = [_softmax_attend(nq_ref[hd], nk_ref[hd], nv_ref[hd]) for hd in range(NA_HEADS)]
    n_ref[...] = jnp.concatenate(outs, axis=-1)


def _ctx_attn_call(aq, ak, av, nq, nk, nv, batch, seq_len):
    t = batch * seq_len
    hspec = lambda n: pl.BlockSpec((n, seq_len, HEAD_DIM), lambda b: (0, b, 0))
    ospec = pl.BlockSpec((seq_len, 256), lambda b: (b, 0))
    return pl.pallas_call(
        _ctx_attn_kernel,
        out_shape=(jax.ShapeDtypeStruct((t, 256), F32), jax.ShapeDtypeStruct((t, 256), F32)),
        grid=(batch,),
        in_specs=[hspec(4), hspec(2), hspec(2), hspec(4), hspec(4), hspec(4)],
        out_specs=(ospec, ospec),
        compiler_params=_params("parallel"),
        name="ctx_attention",
    )(aq, ak, av, nq, nk, nv)


def _lat_attn_kernel(q_ref, k_ref, v_ref, kc_ref, vc_ref, o_ref):
    k = k_ref[...]
    v = v_ref[...]
    kc = kc_ref[...]
    vc = vc_ref[...]
    outs = []
    for g in range(2):
        q = q_ref[g] * ATTN_SCALE
        s_l = _dot_nt(q, k)
        s_c = _dot_nt(q, kc)
        m = jnp.maximum(jnp.max(s_l, axis=-1, keepdims=True), jnp.max(s_c, axis=-1, keepdims=True))
        p_l = jnp.exp(s_l - m)
        p_c = jnp.exp(s_c - m)
        den = jnp.sum(p_l, axis=-1, keepdims=True) + jnp.sum(p_c, axis=-1, keepdims=True)
        outs.append((_dot(p_l, v) + _dot(p_c, vc)) / den)
    o_ref[...] = jnp.concatenate(outs, axis=-1)


def _lat_attn_call(aq, ak, av, kc, vc, batch, seq_len):
    t = batch * seq_len
    tq = ATTN_Q_TILE
    nqb = seq_len // tq
    return pl.pallas_call(
        _lat_attn_kernel,
        out_shape=jax.ShapeDtypeStruct((t, 256), F32),
        grid=(batch, ATTN_KV_HEADS, nqb),
        in_specs=[
            pl.BlockSpec((2, tq, HEAD_DIM), lambda b, g, i: (g, b * nqb + i, 0)),
            pl.BlockSpec((None, seq_len, HEAD_DIM), lambda b, g, i: (g, b, 0)),
            pl.BlockSpec((None, seq_len, HEAD_DIM), lambda b, g, i: (g, b, 0)),
            pl.BlockSpec((None, None, kc.shape[2], HEAD_DIM), lambda b, g, i: (b, g, 0, 0)),
            pl.BlockSpec((None, None, kc.shape[2], HEAD_DIM), lambda b, g, i: (b, g, 0, 0)),
        ],
        out_specs=pl.BlockSpec((tq, 128), lambda b, g, i: (b * nqb + i, g)),
        compiler_params=_params("parallel", "parallel", "arbitrary"),
        name="lat_attention",
    )(aq, ak, av, kc, vc)


def _na_geometry(rows):
    kh = min(NA_KH, rows)
    gr = NA_GROUP_ROWS
    kb = min(gr + kh, rows)
    starts, classes, tables = [], [], []
    for g in range(rows // gr):
        us = int(np.clip(gr * g - kh // 2, 0, rows - kb))
        rel = np.full((gr, kb), -1, np.int64)
        for dr in range(gr):
            r = gr * g + dr
            rs = int(np.clip(r - kh // 2, 0, rows - kh))
            assert us <= rs and rs + kh <= us + kb
            for k in range(kh):
                rel[dr, rs + k - us] = rs + k - r + (NA_KH - 1)
        for ci, tab in enumerate(tables):
            if np.array_equal(tab, rel):
                classes.append(ci)
                break
        else:
            classes.append(len(tables))
            tables.append(rel)
        starts.append(us)
    return kb, starts, classes, tables


def _na_kernel(start_ref, class_ref, q_ref, k_ref, v_ref, kc_ref, vc_ref, bias_ref, o_ref, *,
               rows, key_rows):
    gq = NA_GROUP_ROWS * GRID_W
    gk = key_rows * GRID_W

    def one_group(g, carry):
        q_off = pl.multiple_of(g * gq, gq)
        k_off = pl.multiple_of(start_ref[g] * GRID_W, GRID_W)
        cls = class_ref[g]
        outs = []
        for hd in range(2):
            q = q_ref[hd, pl.ds(q_off, gq), :] * ATTN_SCALE
            kw = k_ref[hd, pl.ds(k_off, gk), :]
            vw = v_ref[hd, pl.ds(k_off, gk), :]
            s_w = _dot_nt(q, kw) + bias_ref[hd, cls]
            s_c = _dot_nt(q, kc_ref[hd])
            m = jnp.maximum(jnp.max(s_w, axis=-1, keepdims=True),
                            jnp.max(s_c, axis=-1, keepdims=True))
            p_w = jnp.exp(s_w - m)
            p_c = jnp.exp(s_c - m)
            den = jnp.sum(p_w, axis=-1, keepdims=True) + jnp.sum(p_c, axis=-1, keepdims=True)
            outs.append((_dot(p_w, vw) + _dot(p_c, vc_ref[hd])) / den)
        o_ref[pl.ds(q_off, gq), :] = jnp.concatenate(outs, axis=-1)
        return carry

    lax.fori_loop(0, rows // NA_GROUP_ROWS, one_group, 0)


def _na_call(nq, nk, nv, kc, vc, bias, batch, seq_len):
    t = batch * seq_len
    rows = seq_len // GRID_W
    key_rows, starts, classes, _ = _na_geometry(rows)
    lspec = pl.BlockSpec((2, seq_len, HEAD_DIM), lambda b, hp, *_: (hp, b, 0))
    cspec = pl.BlockSpec((None, 2, kc.shape[2], HEAD_DIM), lambda b, hp, *_: (b, hp, 0, 0))
    return pl.pallas_call(
        functools.partial(_na_kernel, rows=rows, key_rows=key_rows),
        out_shape=jax.ShapeDtypeStruct((t, 256), F32),
        grid_spec=pltpu.PrefetchScalarGridSpec(
            num_scalar_prefetch=2,
            grid=(batch, NA_HEADS // 2),
            in_specs=[lspec, lspec, lspec, cspec, cspec,
                      pl.BlockSpec((2,) + bias.shape[1:], lambda b, hp, *_: (hp, 0, 0, 0))],
            out_specs=pl.BlockSpec((seq_len, 128), lambda b, hp, *_: (b, hp)),
        ),
        compiler_params=_params("parallel", "parallel"),
        name="neighbourhood_attention",
    )(jnp.asarray(starts, jnp.int32), jnp.asarray(classes, jnp.int32), nq, nk, nv, kc, vc, bias)


def _na_bias_patterns(bias_table, rows):
    key_rows, _, _, tables = _na_geometry(rows)
    rel = np.stack(tables)
    c = np.arange(GRID_W)
    cs = np.clip(c - NA_KW // 2, 0, GRID_W - NA_KW)
    kc = np.arange(GRID_W)
    inside = (kc[None, :] >= cs[:, None]) & (kc[None, :] < cs[:, None] + NA_KW)
    col_rel = np.clip(kc[None, :] - c[:, None] + (NA_KW - 1), 0, 2 * NA_KW - 2)
    col_pick = (col_rel[:, :, None] == np.arange(2 * NA_KW - 1)).astype(np.float32)
    blocks = jnp.einsum("hrs,cqs->hrcq", bias_table, col_pick, precision=lax.Precision.HIGHEST)
    blocks = jnp.where(inside[None, None], blocks, NEG_BIG)
    masked = jnp.full(blocks[:, 0].shape, NEG_BIG, F32)
    per_class = []
    for tab in rel:
        per_row = [jnp.concatenate([blocks[:, ro] if ro >= 0 else masked for ro in row], axis=-1)
                   for row in tab]
        per_class.append(jnp.concatenate(per_row, axis=1))
    return jnp.stack(per_class, axis=1)


def _fourier_kernel(x_ref, cl_ref, sl_ref, cc_ref, sc_ref, w_ref, o_ref, *, norm):
    x = x_ref[...]
    xc = _dot(x, cc_ref[...])
    xs = _dot(x, sc_ref[...])
    y = (_dot(cl_ref[...], xc) - _dot(sl_ref[...], xs)) * norm
    o_ref[...] = _dot(y, w_ref[...])


def _fourier_call(fx, w_fourier, batch, seq_len):
    t = batch * seq_len
    tm = min(DFT_ROW_TILE, seq_len)
    nmb = seq_len // tm
    j = np.arange(seq_len, dtype=np.float64)
    ang_l = 2.0 * np.pi * ((j[:, None] * j[None, :]) % seq_len) / seq_len
    hw = GROUP_WIDTH // FOURIER_HEADS
    c = np.arange(hw, dtype=np.float64)
    ang_c = 2.0 * np.pi * ((c[:, None] * c[None, :]) % hw) / hw
    eye = np.eye(FOURIER_HEADS)
    cl = jnp.asarray(np.cos(ang_l), F32)
    sl = jnp.asarray(np.sin(ang_l), F32)
    cc = jnp.asarray(np.kron(eye, np.cos(ang_c)), F32)
    sc = jnp.asarray(np.kron(eye, np.sin(ang_c)), F32)
    const = lambda shape: pl.BlockSpec(shape, lambda i, b: (0, 0))
    return pl.pallas_call(
        functools.partial(_fourier_kernel, norm=float((seq_len * hw) ** -0.5)),
        out_shape=jax.ShapeDtypeStruct((t, 256), F32),
        grid=(nmb, batch),
        in_specs=[
            pl.BlockSpec((seq_len, 256), lambda i, b: (b, 0)),
            pl.BlockSpec((tm, seq_len), lambda i, b: (i, 0)),
            pl.BlockSpec((tm, seq_len), lambda i, b: (i, 0)),
            const((256, 256)), const((256, 256)), const((256, 256)),
        ],
        out_specs=pl.BlockSpec((tm, 256), lambda i, b: (b * nmb + i, 0)),
        compiler_params=_params("parallel", "arbitrary"),
        name="fourier_mix",
    )(fx, cl, sl, cc, sc, w_fourier)


POOL_PAD = 8


def _pool_kernel(x_ref, w_ref, s_ref, o_ref, pad_ref, *, seq_len):
    x = x_ref[...]
    zeros = jnp.zeros((POOL_PAD, GROUP_WIDTH), F32)
    pad_ref[0:POOL_PAD, :] = zeros
    pad_ref[POOL_PAD + seq_len:2 * POOL_PAD + seq_len, :] = zeros
    pad_ref[POOL_PAD:POOL_PAD + seq_len, :] = x

    def shifted(off):
        return pad_ref[POOL_PAD + off:POOL_PAD + off + seq_len, :]

    t = lax.broadcasted_iota(jnp.int32, (seq_len, GROUP_WIDTH), 0)
    lane_group = lax.broadcasted_iota(jnp.int32, (seq_len, GROUP_WIDTH), 1) // POOL_GROUP_W
    acc = x
    done = 0
    pooled = jnp.zeros_like(x)
    for g, w in enumerate(POOL_WINDOWS):
        half = w // 2
        for off in list(range(-half, -done)) + list(range(max(done, 1), half)):
            acc = acc + shifted(off)
        done = half
        cnt = jnp.minimum(t + half, seq_len) - jnp.maximum(t - half, 0)
        pooled = jnp.where(lane_group == g, acc / cnt.astype(F32), pooled)
    o_ref[...] = _dot(pooled - x, w_ref[...]) * s_ref[...]


def _pool_call(px, w_pool_bd, pool_scale, batch, seq_len):
    t = batch * seq_len
    return pl.pallas_call(
        functools.partial(_pool_kernel, seq_len=seq_len),
        out_shape=jax.ShapeDtypeStruct((t, 256), F32),
        grid=(batch,),
        in_specs=[
            pl.BlockSpec((seq_len, 256), lambda b: (b, 0)),
            pl.BlockSpec((256, 256), lambda b: (0, 0)),
            pl.BlockSpec((1, 256), lambda b: (0, 0)),
        ],
        out_specs=pl.BlockSpec((seq_len, 256), lambda b: (b, 0)),
        scratch_shapes=[pltpu.VMEM((seq_len + 2 * POOL_PAD, GROUP_WIDTH), F32)],
        compiler_params=_params("parallel"),
        name="multiscale_pool",
    )(px, w_pool_bd, pool_scale)


def _outproj_kernel(a_ref, f_ref, n_ref, p_ref, x_ref, ada_ref, w_ref, g_ref, b_ref, o_ref):
    mix = (_dot(a_ref[...], w_ref[0:256, :]) + _dot(f_ref[...], w_ref[256:512, :])
           + _dot(n_ref[...], w_ref[512:768, :]) + _dot(p_ref[...], w_ref[768:1024, :]))
    gate = ada_ref[:, 2 * D_MODEL:3 * D_MODEL]
    z = ALPHA * x_ref[...] + gate * mix
    o_ref[...] = _layer_norm(z) * g_ref[...] + b_ref[...]


def _outproj_call(a_out, f_out, n_out, p_out, x, ada, cond_base, cond_span, w_out, ln_g, ln_b):
    t = x.shape[0]
    tm = TOKEN_TILE
    tiles_per_cond = cond_span // tm
    part = pl.BlockSpec((tm, 256), lambda i: (i, 0))
    row = pl.BlockSpec((1, D_MODEL), lambda i: (0, 0))
    return pl.pallas_call(
        _outproj_kernel,
        out_shape=jax.ShapeDtypeStruct((t, D_MODEL), F32),
        grid=(t // tm,),
        in_specs=[part, part, part, part,
                  pl.BlockSpec((tm, D_MODEL), lambda i: (i, 0)),
                  pl.BlockSpec((None, 1, ADA_CHUNKS * D_MODEL),
                               lambda i: (cond_base + i // tiles_per_cond, 0, 0)),
                  pl.BlockSpec((D_MODEL, D_MODEL), lambda i: (0, 0)), row, row],
        out_specs=pl.BlockSpec((tm, D_MODEL), lambda i: (i, 0)),
        compiler_params=_params("parallel"),
        name="outproj_postnorm",
    )(a_out, f_out, n_out, p_out, x, ada, w_out, ln_g, ln_b)


def _sorting_network(n):
    pairs = []
    p = 1
    while p < n:
        k = p
        while k >= 1:
            for j in range(k % p, n - k, 2 * k):
                for i in range(min(k, n - j - k)):
                    if (i + j) // (2 * p) == (i + j + k) // (2 * p):
                        pairs.append((i + j, i + j + k))
            k //= 2
        p *= 2
    return pairs


def _sort_descending(rows):
    rows = list(rows)
    for a, b in _sorting_network(len(rows)):
        rows[a], rows[b] = jnp.maximum(rows[a], rows[b]), jnp.minimum(rows[a], rows[b])
    return rows


def _sublane_max_all(x):
    x = jnp.maximum(x, pltpu.roll(x, 4, axis=0))
    x = jnp.maximum(x, pltpu.roll(x, 2, axis=0))
    return jnp.maximum(x, pltpu.roll(x, 1, axis=0))


def _extract_top(groups, count):
    groups = [list(g) for g in groups]
    out = []
    for it in range(count):
        head = groups[0][0]
        for g in groups[1:]:
            head = jnp.maximum(head, g[0])
        m = _sublane_max_all(head)
        out.append(m)
        remaining = count - it - 1
        if remaining == 0:
            break
        for g in groups:
            eq = g[0] == m
            keep = min(len(g), remaining)
            for i in range(keep):
                nxt = g[i + 1] if i + 1 < len(g) else jnp.full_like(m, -jnp.inf)
                g[i] = jnp.where(eq, nxt, g[i])
            del g[keep:]
    return out


def _peer_route(s1, s2):
    groups = PEER_N_KEYS // 8
    s1_rows = [s1[8 * i:8 * (i + 1)] for i in range(groups)]
    s2_rows = [s2[8 * i:8 * (i + 1)] for i in range(groups)]
    v1 = _extract_top([_sort_descending(s1_rows)], PEER_TOPK + 1)
    v2 = _extract_top([_sort_descending(s2_rows)], PEER_TOPK + 1)
    sub = lax.broadcasted_iota(jnp.int32, v1[0].shape, 0)
    lo = v1[0]
    hi = v1[8]
    for r in range(1, 8):
        lo = jnp.where(sub == r, v1[r], lo)
        hi = jnp.where(sub == r, v1[8 + r], hi)
    cand = [[lo + v2[b] for b in range(PEER_TOPK + 1)], [hi + v2[0]], [v1[PEER_TOPK] + v2[0]]]
    top = _extract_top(cand, PEER_TOPK + 1)
    den = jnp.ones_like(top[0])
    for tk in top[1:PEER_TOPK]:
        den = den + jnp.exp(tk - top[0])
    half_inv_den = 0.5 / den
    tau = 0.5 * (top[PEER_TOPK - 1] + top[PEER_TOPK])
    thr = [tau - r for r in s1_rows]
    coef = [jnp.exp(r - v1[0]) * half_inv_den for r in s1_rows]
    e2 = [jnp.exp(r - v2[0]) for r in s2_rows]
    return thr, coef, e2


def _peer_kernel(x_ref, ada_ref, wq_ref, k1_ref, k2_ref, u_ref, vt_ref, g_ref, b_ref, o_ref,
                 h_ref, ht_ref, yt_ref, act_ref, s2_ref, e2_ref, thr_ref, coef_ref):
    j = pl.program_id(1)
    tt = x_ref.shape[0]
    half = PEER_KEY_DIM // 2
    lanes = PEER_LANE_CHUNK

    @pl.when(j == 0)
    def _():
        sh = ada_ref[:, 3 * D_MODEL:4 * D_MODEL]
        sc = ada_ref[:, 4 * D_MODEL:5 * D_MODEL]
        h = _layer_norm(x_ref[...]) * (1.0 + sc) + sh
        h_ref[...] = h.astype(BF16)
        ht_ref[...] = h.T.astype(BF16)
        yt_ref[...] = jnp.zeros_like(yt_ref)

        def one_head(hd):
            w_rows = wq_ref[pl.ds(pl.multiple_of(hd * PEER_KEY_DIM, PEER_KEY_DIM), PEER_KEY_DIM), :]
            q_t = lax.dot_general(w_rows, h_ref[...], (((1,), (1,)), ((), ())),
                                  preferred_element_type=F32)
            s1 = _dot(k1_ref[...], q_t[0:half])
            s2 = _dot(k2_ref[...], q_t[half:2 * half])
            s2_ref[hd] = s2
            for c in range(tt // lanes):
                cols = slice(c * lanes, (c + 1) * lanes)
                thr, coef, e2 = _peer_route(s1[:, cols], s2[:, cols])
                for i in range(PEER_N_KEYS // 8):
                    rows = pl.ds(8 * i, 8)
                    thr_ref[hd, rows, cols] = thr[i]
                    coef_ref[hd, rows, cols] = coef[i]
                    e2_ref[hd, rows, cols] = e2[i]

        def head_group(g, carry):
            for k in range(PEER_HEADS_PER_ITER):
                one_head(g * PEER_HEADS_PER_ITER + k)
            return carry

        lax.fori_loop(0, PEER_HEADS // PEER_HEADS_PER_ITER, head_group, 0)

    keys_per_block = PEER_EXPERT_TILE // PEER_N_KEYS
    keys_per_sub = PEER_SUB_TILE // PEER_N_KEYS
    n_sub = PEER_EXPERT_TILE // PEER_SUB_TILE

    def up_proj(sub):
        return jnp.dot(u_ref[sub * PEER_SUB_TILE:(sub + 1) * PEER_SUB_TILE, :], ht_ref[...],
                       preferred_element_type=F32)

    acc = None
    a_next = up_proj(0)
    for sub in range(n_sub):
        sub_rows = slice(sub * PEER_SUB_TILE, (sub + 1) * PEER_SUB_TILE)
        a_sub = a_next
        if sub + 1 < n_sub:
            a_next = up_proj(sub + 1)
        for ii in range(keys_per_sub):
            i1 = j * keys_per_block + sub * keys_per_sub + ii
            rows = slice(ii * PEER_N_KEYS, (ii + 1) * PEER_N_KEYS)
            thr_rows = [thr_ref[hd, pl.ds(i1, 1), :] for hd in range(PEER_HEADS)]
            coef_rows = [coef_ref[hd, pl.ds(i1, 1), :] for hd in range(PEER_HEADS)]
            for c in range(tt // lanes):
                cols = slice(c * lanes, (c + 1) * lanes)
                w = None
                for hd in range(PEER_HEADS):
                    thr = thr_rows[hd][:, cols]
                    coef = coef_rows[hd][:, cols]
                    term = jnp.where(s2_ref[hd, :, cols] >= thr, e2_ref[hd, :, cols], 0.0) * coef
                    w = term if w is None else w + term
                a = a_sub[rows, cols]
                gelu2 = a * (1.0 + lax.erf(a * (2.0 ** -0.5)))
                act_ref[sub * PEER_SUB_TILE + ii * PEER_N_KEYS:
                        sub * PEER_SUB_TILE + (ii + 1) * PEER_N_KEYS, cols] = (w * gelu2).astype(BF16)
        part = jnp.dot(vt_ref[:, sub_rows], act_ref[sub_rows, :], preferred_element_type=F32)
        acc = part if acc is None else acc + part
    yt_ref[...] += acc

    @pl.when(j == pl.num_programs(1) - 1)
    def _():
        gate = ada_ref[:, 5 * D_MODEL:6 * D_MODEL]
        z = ALPHA * x_ref[...] + gate * yt_ref[...].T
        o_ref[...] = _layer_norm(z) * g_ref[...] + b_ref[...]


def _peer_call(x, ada, cond_base, cond_span, layer, wq_t, k1, k2, u, v_t, ln_g, ln_b):
    t = x.shape[0]
    tt = PEER_TOKEN_TILE
    tiles_per_cond = cond_span // tt
    eb = PEER_EXPERT_TILE
    row = pl.BlockSpec((1, D_MODEL), lambda i, j: (0, 0))
    per_head = pltpu.VMEM((PEER_HEADS, PEER_N_KEYS, tt), F32)
    return pl.pallas_call(
        _peer_kernel,
        out_shape=jax.ShapeDtypeStruct((t, D_MODEL), F32),
        grid=(t // tt, PEER_N_EXPERTS // eb),
        in_specs=[
            pl.BlockSpec((tt, D_MODEL), lambda i, j: (i, 0)),
            pl.BlockSpec((None, 1, ADA_CHUNKS * D_MODEL),
                         lambda i, j: (cond_base + i // tiles_per_cond, 0, 0)),
            pl.BlockSpec((PEER_HEADS * PEER_KEY_DIM, D_MODEL), lambda i, j: (0, 0)),
            pl.BlockSpec((PEER_N_KEYS, PEER_KEY_DIM // 2), lambda i, j: (0, 0)),
            pl.BlockSpec((PEER_N_KEYS, PEER_KEY_DIM // 2), lambda i, j: (0, 0)),
            pl.BlockSpec((None, eb, D_MODEL), lambda i, j: (layer, j, 0)),
            pl.BlockSpec((None, D_MODEL, eb), lambda i, j: (layer, 0, j)),
            row, row,
        ],
        out_specs=pl.BlockSpec((tt, D_MODEL), lambda i, j: (i, 0)),
        scratch_shapes=[
            pltpu.VMEM((tt, D_MODEL), BF16),
            pltpu.VMEM((D_MODEL, tt), BF16),
            pltpu.VMEM((D_MODEL, tt), F32),
            pltpu.VMEM((eb, tt), BF16),
            per_head, per_head, per_head, per_head,
        ],
        compiler_params=_params("parallel", "arbitrary"),
        name="peer",
    )(x, ada, wq_t, k1, k2, u, v_t, ln_g, ln_b)


def _rope_tables(seq_len):
    t = np.arange(seq_len)
    row = (t // GRID_W).astype(np.float64)
    col = (t % GRID_W).astype(np.float64)
    nf = HEAD_DIM // 4
    inv = ROPE_THETA ** (-np.arange(nf, dtype=np.float64) / nf)
    d = np.arange(HEAD_DIM)
    pos = np.where(d[None, :] < HEAD_DIM // 2, row[:, None], col[:, None])
    ang = pos * inv[d % nf][None, :]
    sign = np.where((d % (HEAD_DIM // 2)) < nf, -1.0, 1.0)
    cos = np.tile(np.cos(ang), (1, ATTN_HEADS))
    sin = np.tile(np.sin(ang) * sign[None, :], (1, ATTN_HEADS))
    return jnp.asarray(cos, F32), jnp.asarray(sin, F32)


def _head_mean_matrix():
    m = np.kron(np.eye(ATTN_HEADS), np.full((HEAD_DIM, HEAD_DIM), 1.0 / HEAD_DIM))
    return jnp.asarray(m, BF16)


def _block_diag(w):
    g, a, b = w.shape
    out = jnp.zeros((g * a, g * b), w.dtype)
    for i in range(g):
        out = out.at[i * a:(i + 1) * a, i * b:(i + 1) * b].set(w[i])
    return out


def _trunk_layer(x, ada, cond_base, batch, seq_len, lw, ctx_cache):
    rope = ctx_cache is not None
    if rope:
        cos_t, sin_t = _rope_tables(seq_len)
    else:
        cos_t = jnp.ones((TOKEN_TILE, 256), F32)
        sin_t = jnp.zeros((TOKEN_TILE, 256), F32)
    cond_span = seq_len if rope else batch * seq_len
    aq, ak, av, fx, nq, nk, nv, px = _inproj_call(
        x, ada, cond_base, cond_span, seq_len, lw["w_in"], lw["gq"], lw["gk"], lw["head_mean"],
        cos_t, sin_t, rope)
    if ctx_cache is None:
        a_out, n_out = _ctx_attn_call(aq, ak, av, nq, nk, nv, batch, seq_len)
    else:
        ck_a, cv_a, ck_n, cv_n = ctx_cache
        a_out = _lat_attn_call(aq, ak, av, ck_a, cv_a, batch, seq_len)
        n_out = _na_call(nq, nk, nv, ck_n, cv_n, lw["na_bias"], batch, seq_len)
    f_out = _fourier_call(fx, lw["w_fourier"], batch, seq_len)
    p_out = _pool_call(px, lw["w_pool"], lw["pool_scale"], batch, seq_len)
    x = _outproj_call(a_out, f_out, n_out, p_out, x, ada, cond_base, cond_span,
                      lw["w_out"], lw["ln1_g"], lw["ln1_b"])
    x = _peer_call(x, ada, cond_base, cond_span, lw["layer"], lw["wq_t"], lw["k1"], lw["k2"],
                   lw["u"], lw["v_t"], lw["ln2_g"], lw["ln2_b"])
    return x, (ak, av, nk, nv)


def kernel(x_prompt, x_sample, cache_attn_k, cache_attn_v, cache_na_k, cache_na_v, c, c_ctx,
           w_ada, b_ada, w_in, q_norm_g, k_norm_g, w_fourier, na_bias, w_pool, pool_scale,
           w_out, ln1_g, ln1_b, ln2_g, ln2_b, peer_wq, peer_k1, peer_k2, peer_u, peer_v):
    batch, seq, d = x_prompt.shape
    dec_batch, dec_seq, _ = x_sample.shape
    xp = x_prompt.reshape(batch * seq, d)
    xs = x_sample.reshape(dec_batch * dec_seq, d)

    cond = jnp.zeros((COND_ROWS, d), F32).at[0].set(c_ctx).at[1:1 + dec_batch].set(c)
    ada_all = _ada_call(cond, w_ada, b_ada)
    head_mean = _head_mean_matrix()
    rows = dec_seq // GRID_W

    def heads_first(cache):
        return cache.transpose(1, 0, 3, 2, 4)

    cak, cav = heads_first(cache_attn_k), heads_first(cache_attn_v)
    cnk, cnv = heads_first(cache_na_k), heads_first(cache_na_v)

    u_all = peer_u.astype(BF16)
    v_t_all = jnp.swapaxes(peer_v.astype(BF16), 1, 2)

    new_kv = [[], [], [], []]
    for l in range(DEPTH):
        lw = {
            "layer": l,
            "w_in": w_in[l].astype(BF16),
            "gq": jnp.tile(q_norm_g[l], ATTN_HEADS).reshape(1, -1),
            "gk": jnp.tile(k_norm_g[l], ATTN_KV_HEADS).reshape(1, -1),
            "head_mean": head_mean,
            "w_fourier": w_fourier[l].astype(BF16),
            "na_bias": _na_bias_patterns(na_bias[l], rows),
            "w_pool": _block_diag(w_pool[l]).astype(BF16),
            "pool_scale": pool_scale[l].reshape(1, -1),
            "w_out": w_out[l].astype(BF16),
            "ln1_g": ln1_g[l].reshape(1, -1), "ln1_b": ln1_b[l].reshape(1, -1),
            "ln2_g": ln2_g[l].reshape(1, -1), "ln2_b": ln2_b[l].reshape(1, -1),
            "wq_t": peer_wq[l].T.astype(BF16),
            "k1": peer_k1[l].astype(BF16), "k2": peer_k2[l].astype(BF16),
            "u": u_all,
            "v_t": v_t_all,
        }
        ada = ada_all[l].reshape(COND_ROWS, 1, ADA_CHUNKS * d)
        xp, kv = _trunk_layer(xp, ada, 0, batch, seq, lw, None)
        for dst, arr in zip(new_kv, kv):
            heads = arr.shape[0]
            dst.append(arr.reshape(heads, batch, seq, HEAD_DIM).transpose(1, 2, 0, 3))
        xs, _ = _trunk_layer(xs, ada, 1, dec_batch, dec_seq, lw,
                             (cak[l], cav[l], cnk[l], cnv[l]))
    outs = [jnp.stack(lst, axis=1) for lst in new_kv]
    return (xp.reshape(batch, seq, d), xs.reshape(dec_batch, dec_seq, d), *outs)
```

```python
import functools
import math

import numpy as np
import jax
import jax.numpy as jnp
from jax import lax
from jax.experimental import pallas as pl
from jax.experimental.pallas import tpu as pltpu

D_MODEL = 1024
DEPTH = 2
GRID_W = 64
HEAD_DIM = 64
GROUP_WIDTH = 256
ATTN_HEADS = 4
ATTN_KV_HEADS = 2
NA_HEADS = 4
NA_KH = 8
NA_KW = 16
FOURIER_HEADS = 4
POOL_WINDOWS = (2, 4, 8, 16)
POOL_GROUP_W = GROUP_WIDTH // len(POOL_WINDOWS)
ROPE_THETA = 10000.0
ATTN_SCALE = HEAD_DIM ** -0.5
PEER_HEADS = 8
PEER_N_KEYS = 128
PEER_N_EXPERTS = PEER_N_KEYS * PEER_N_KEYS
PEER_KEY_DIM = 256
PEER_TOPK = 16
ALPHA = (2 * DEPTH) ** 0.25
LN_EPS = 1e-6
RMS_EPS = 1e-6
ADA_CHUNKS = 6
P_IN = 1792
COND_ROWS = 8

TOKEN_TILE = 512
ATTN_Q_TILE = 512
DFT_ROW_TILE = 512
NA_GROUP_ROWS = 4
PEER_TOKEN_TILE = 512
PEER_EXPERT_TILE = 2048
PEER_SUB_TILE = 256
PEER_LANE_CHUNK = 128
PEER_HEADS_PER_ITER = 8
ADA_COL_TILE = 2048
NEG_BIG = -1e30
VMEM_LIMIT = 56 * 1024 * 1024

F32 = jnp.float32
BF16 = jnp.bfloat16


def _params(*sem):
    return pltpu.CompilerParams(dimension_semantics=sem, vmem_limit_bytes=VMEM_LIMIT)


def _dot(a, b):
    return jnp.dot(a.astype(BF16), b.astype(BF16), preferred_element_type=F32)


def _dot_nt(a, b):
    return lax.dot_general(a.astype(BF16), b.astype(BF16), (((1,), (1,)), ((), ())),
                           preferred_element_type=F32)


def _layer_norm(x):
    mu = jnp.mean(x, axis=-1, keepdims=True)
    xc = x - mu
    var = jnp.mean(xc * xc, axis=-1, keepdims=True)
    return xc * lax.rsqrt(var + LN_EPS)


def _ada_kernel(cond_ref, w_ref, b_ref, o_ref):
    c = cond_ref[...]
    s = c / (1.0 + jnp.exp(-c))
    o_ref[...] = jnp.dot(s, w_ref[...], precision=lax.Precision.HIGHEST,
                         preferred_element_type=F32) + b_ref[...]


def _ada_call(cond, w_ada, b_ada):
    n = ADA_CHUNKS * D_MODEL
    return pl.pallas_call(
        _ada_kernel,
        out_shape=jax.ShapeDtypeStruct((DEPTH, COND_ROWS, n), F32),
        grid=(DEPTH, n // ADA_COL_TILE),
        in_specs=[
            pl.BlockSpec((COND_ROWS, D_MODEL), lambda l, j: (0, 0)),
            pl.BlockSpec((None, D_MODEL, ADA_COL_TILE), lambda l, j: (l, 0, j)),
            pl.BlockSpec((None, 1, ADA_COL_TILE), lambda l, j: (l, 0, j)),
        ],
        out_specs=pl.BlockSpec((None, COND_ROWS, ADA_COL_TILE), lambda l, j: (l, 0, j)),
        compiler_params=_params("parallel", "arbitrary"),
        name="ada_proj",
    )(cond, w_ada, b_ada.reshape(DEPTH, 1, n))


def _head_mean_sq(x, g):
    sq = x * x
    hi = sq.astype(BF16)
    lo = (sq - hi.astype(F32)).astype(BF16)
    return (jnp.dot(hi, g, preferred_element_type=F32)
            + jnp.dot(lo, g, preferred_element_type=F32))


def _rotary(x, cos, sin_signed):
    w = x.shape[-1]
    fwd = pltpu.roll(x, w - 16, axis=1)
    bwd = pltpu.roll(x, 16, axis=1)
    lane = lax.broadcasted_iota(jnp.int32, x.shape, 1)
    partner = jnp.where((lane % 32) < 16, fwd, bwd)
    return x * cos + partner * sin_signed


def _inproj_kernel(x_ref, ada_ref, w_ref, gq_ref, gk_ref, hm_ref, cos_ref, sin_ref,
                   aq_ref, ak_ref, av_ref, fx_ref, nq_ref, nk_ref, nv_ref, px_ref, *, rope):
    x = x_ref[...]
    sh = ada_ref[:, 0:D_MODEL]
    sc = ada_ref[:, D_MODEL:2 * D_MODEL]
    h = _layer_norm(x) * (1.0 + sc) + sh
    proj = _dot(h, w_ref[...])
    aq = proj[:, 0:256]
    ak = proj[:, 256:384]
    hm = hm_ref[...]
    aq = aq * lax.rsqrt(_head_mean_sq(aq, hm) + RMS_EPS) * gq_ref[...]
    ak = ak * lax.rsqrt(_head_mean_sq(ak, hm[0:128, 0:128]) + RMS_EPS) * gk_ref[...]
    if rope:
        cos = cos_ref[...]
        sin = sin_ref[...]
        aq = _rotary(aq, cos, sin)
        ak = _rotary(ak, cos[:, 0:128], sin[:, 0:128])
    for hd in range(ATTN_HEADS):
        aq_ref[hd] = aq[:, hd * 64:(hd + 1) * 64]
    for hd in range(ATTN_KV_HEADS):
        ak_ref[hd] = ak[:, hd * 64:(hd + 1) * 64]
        av_ref[hd] = proj[:, 384 + hd * 64:384 + (hd + 1) * 64]
    fx_ref[...] = proj[:, 512:768]
    for hd in range(NA_HEADS):
        nq_ref[hd] = proj[:, 768 + hd * 64:768 + (hd + 1) * 64]
        nk_ref[hd] = proj[:, 1024 + hd * 64:1024 + (hd + 1) * 64]
        nv_ref[hd] = proj[:, 1280 + hd * 64:1280 + (hd + 1) * 64]
    px_ref[...] = proj[:, 1536:1792]


def _inproj_call(x, ada, cond_base, cond_span, seq_len, w_in, gq, gk, head_mean, cos_t, sin_t,
                 rope):
    t = x.shape[0]
    tm = TOKEN_TILE
    tiles_per_seq = seq_len // tm if rope else 1
    tiles_per_cond = cond_span // tm
    hm = lambda n: jax.ShapeDtypeStruct((n, t, HEAD_DIM), F32)
    hspec = lambda n: pl.BlockSpec((n, tm, HEAD_DIM), lambda i: (0, i, 0))
    const = lambda shape: pl.BlockSpec(shape, lambda i: tuple(0 for _ in shape))
    return pl.pallas_call(
        functools.partial(_inproj_kernel, rope=rope),
        out_shape=(hm(4), hm(2), hm(2), jax.ShapeDtypeStruct((t, 256), F32),
                   hm(4), hm(4), hm(4), jax.ShapeDtypeStruct((t, 256), F32)),
        grid=(t // tm,),
        in_specs=[
            pl.BlockSpec((tm, D_MODEL), lambda i: (i, 0)),
            pl.BlockSpec((None, 1, ADA_CHUNKS * D_MODEL),
                         lambda i: (cond_base + i // tiles_per_cond, 0, 0)),
            const((D_MODEL, P_IN)),
            const((1, 256)), const((1, 128)), const((256, 256)),
            pl.BlockSpec((tm, 256), lambda i: (i % tiles_per_seq, 0)),
            pl.BlockSpec((tm, 256), lambda i: (i % tiles_per_seq, 0)),
        ],
        out_specs=(hspec(4), hspec(2), hspec(2), pl.BlockSpec((tm, 256), lambda i: (i, 0)),
                   hspec(4), hspec(4), hspec(4), pl.BlockSpec((tm, 256), lambda i: (i, 0))),
        compiler_params=_params("parallel"),
        name="inproj_rope" if rope else "inproj",
    )(x, ada, w_in, gq, gk, head_mean, cos_t, sin_t)


def _softmax_attend(q, k, v):
    s = _dot_nt(q * ATTN_SCALE, k)
    m = jnp.max(s, axis=-1, keepdims=True)
    p = jnp.exp(s - m)
    return _dot(p, v) / jnp.sum(p, axis=-1, keepdims=True)


def _ctx_attn_kernel(aq_ref, ak_ref, av_ref, nq_ref, nk_ref, nv_ref, a_ref, n_ref):
    seq = aq_ref.shape[1]
    outs = []
    for g in range(ATTN_KV_HEADS):
        q2 = jnp.concatenate([aq_ref[2 * g], aq_ref[2 * g + 1]], axis=0)
        o2 = _softmax_attend(q2, ak_ref[g], av_ref[g])
        outs += [o2[0:seq], o2[seq:2 * seq]]
    a_ref[...] = jnp.concatenate(outs, axis=-1)
    outs = [_softmax_attend(nq_ref[hd], nk_ref[hd], nv_ref[hd]) for hd in range(NA_HEADS)]
    n_ref[...] = jnp.concatenate(outs, axis=-1)


def _ctx_attn_call(aq, ak, av, nq, nk, nv, batch, seq_len):
    t = batch * seq_len
    hspec = lambda n: pl.BlockSpec((n, seq_len, HEAD_DIM), lambda b: (0, b, 0))
    ospec = pl.BlockSpec((seq_len, 256), lambda b: (b, 0))
    return pl.pallas_call(
        _ctx_attn_kernel,
        out_shape=(jax.ShapeDtypeStruct((t, 256), F32), jax.ShapeDtypeStruct((t, 256), F32)),
        grid=(batch,),
        in_specs=[hspec(4), hspec(2), hspec(2), hspec(4), hspec(4), hspec(4)],
        out_specs=(ospec, ospec),
        compiler_params=_params("parallel"),
        name="ctx_attention",
    )(aq, ak, av, nq, nk, nv)


def _lat_attn_kernel(q_ref, k_ref, v_ref, kc_ref, vc_ref, o_ref):
    k = k_ref[...]
    v = v_ref[...]
    kc = kc_ref[...]
    vc = vc_ref[...]
    outs = []
    for g in range(2):
        q = q_ref[g] * ATTN_SCALE
        s_l = _dot_nt(q, k)
        s_c = _dot_nt(q, kc)
        m = jnp.maximum(jnp.max(s_l, axis=-1, keepdims=True), jnp.max(s_c, axis=-1, keepdims=True))
        p_l = jnp.exp(s_l - m)
        p_c = jnp.exp(s_c - m)
        den = jnp.sum(p_l, axis=-1, keepdims=True) + jnp.sum(p_c, axis=-1, keepdims=True)
        outs.append((_dot(p_l, v) + _dot(p_c, vc)) / den)
    o_ref[...] = jnp.concatenate(outs, axis=-1)


def _lat_attn_call(aq, ak, av, kc, vc, batch, seq_len):
    t = batch * seq_len
    tq = ATTN_Q_TILE
    nqb = seq_len // tq
    return pl.pallas_call(
        _lat_attn_kernel,
        out_shape=jax.ShapeDtypeStruct((t, 256), F32),
        grid=(batch, ATTN_KV_HEADS, nqb),
        in_specs=[
            pl.BlockSpec((2, tq, HEAD_DIM), lambda b, g, i: (g, b * nqb + i, 0)),
            pl.BlockSpec((None, seq_len, HEAD_DIM), lambda b, g, i: (g, b, 0)),
            pl.BlockSpec((None, seq_len, HEAD_DIM), lambda b, g, i: (g, b, 0)),
            pl.BlockSpec((None, None, kc.shape[2], HEAD_DIM), lambda b, g, i: (b, g, 0, 0)),
            pl.BlockSpec((None, None, kc.shape[2], HEAD_DIM), lambda b, g, i: (b, g, 0, 0)),
        ],
        out_specs=pl.BlockSpec((tq, 128), lambda b, g, i: (b * nqb + i, g)),
        compiler_params=_params("parallel", "parallel", "arbitrary"),
        name="lat_attention",
    )(aq, ak, av, kc, vc)


def _na_geometry(rows):
    kh = min(NA_KH, rows)
    gr = NA_GROUP_ROWS
    kb = min(gr + kh, rows)
    starts, classes, tables = [], [], []
    for g in range(rows // gr):
        us = int(np.clip(gr * g - kh // 2, 0, rows - kb))
        rel = np.full((gr, kb), -1, np.int64)
        for dr in range(gr):
            r = gr * g + dr
            rs = int(np.clip(r - kh // 2, 0, rows - kh))
            assert us <= rs and rs + kh <= us + kb
            for k in range(kh):
                rel[dr, rs + k - us] = rs + k - r + (NA_KH - 1)
        for ci, tab in enumerate(tables):
            if np.array_equal(tab, rel):
                classes.append(ci)
                break
        else:
            classes.append(len(tables))
            tables.append(rel)
        starts.append(us)
    return kb, starts, classes, tables


def _na_kernel(start_ref, class_ref, q_ref, k_ref, v_ref, kc_ref, vc_ref, bias_ref, o_ref, *,
               rows, key_rows):
    gq = NA_GROUP_ROWS * GRID_W
    gk = key_rows * GRID_W

    def one_group(g, carry):
        q_off = pl.multiple_of(g * gq, gq)
        k_off = pl.multiple_of(start_ref[g] * GRID_W, GRID_W)
        cls = class_ref[g]
        outs = []
        for hd in range(2):
            q = q_ref[hd, pl.ds(q_off, gq), :] * ATTN_SCALE
            kw = k_ref[hd, pl.ds(k_off, gk), :]
            vw = v_ref[hd, pl.ds(k_off, gk), :]
            s_w = _dot_nt(q, kw) + bias_ref[hd, cls]
            s_c = _dot_nt(q, kc_ref[hd])
            m = jnp.maximum(jnp.max(s_w, axis=-1, keepdims=True),
                            jnp.max(s_c, axis=-1, keepdims=True))
            p_w = jnp.exp(s_w - m)
            p_c = jnp.exp(s_c - m)
            den = jnp.sum(p_w, axis=-1, keepdims=True) + jnp.sum(p_c, axis=-1, keepdims=True)
            outs.append((_dot(p_w, vw) + _dot(p_c, vc_ref[hd])) / den)
        o_ref[pl.ds(q_off, gq), :] = jnp.concatenate(outs, axis=-1)
        return carry

    lax.fori_loop(0, rows // NA_GROUP_ROWS, one_group, 0)


def _na_call(nq, nk, nv, kc, vc, bias, batch, seq_len):
    t = batch * seq_len
    rows = seq_len // GRID_W
    key_rows, starts, classes, _ = _na_geometry(rows)
    lspec = pl.BlockSpec((2, seq_len, HEAD_DIM), lambda b, hp, *_: (hp, b, 0))
    cspec = pl.BlockSpec((None, 2, kc.shape[2], HEAD_DIM), lambda b, hp, *_: (b, hp, 0, 0))
    return pl.pallas_call(
        functools.partial(_na_kernel, rows=rows, key_rows=key_rows),
        out_shape=jax.ShapeDtypeStruct((t, 256), F32),
        grid_spec=pltpu.PrefetchScalarGridSpec(
            num_scalar_prefetch=2,
            grid=(batch, NA_HEADS // 2),
            in_specs=[lspec, lspec, lspec, cspec, cspec,
                      pl.BlockSpec((2,) + bias.shape[1:], lambda b, hp, *_: (hp, 0, 0, 0))],
            out_specs=pl.BlockSpec((seq_len, 128), lambda b, hp, *_: (b, hp)),
        ),
        compiler_params=_params("parallel", "parallel"),
        name="neighbourhood_attention",
    )(jnp.asarray(starts, jnp.int32), jnp.asarray(classes, jnp.int32), nq, nk, nv, kc, vc, bias)


def _na_bias_patterns(bias_table, rows):
    key_rows, _, _, tables = _na_geometry(rows)
    rel = np.stack(tables)
    c = np.arange(GRID_W)
    cs = np.clip(c - NA_KW // 2, 0, GRID_W - NA_KW)
    kc = np.arange(GRID_W)
    inside = (kc[None, :] >= cs[:, None]) & (kc[None, :] < cs[:, None] + NA_KW)
    col_rel = np.clip(kc[None, :] - c[:, None] + (NA_KW - 1), 0, 2 * NA_KW - 2)
    col_pick = (col_rel[:, :, None] == np.arange(2 * NA_KW - 1)).astype(np.float32)
    blocks = jnp.einsum("hrs,cqs->hrcq", bias_table, col_pick, precision=lax.Precision.HIGHEST)
    blocks = jnp.where(inside[None, None], blocks, NEG_BIG)
    masked = jnp.full(blocks[:, 0].shape, NEG_BIG, F32)
    per_class = []
    for tab in rel:
        per_row = [jnp.concatenate([blocks[:, ro] if ro >= 0 else masked for ro in row], axis=-1)
                   for row in tab]
        per_class.append(jnp.concatenate(per_row, axis=1))
    return jnp.stack(per_class, axis=1)


def _fourier_kernel(x_ref, cl_ref, sl_ref, cc_ref, sc_ref, w_ref, o_ref, *, norm):
    x = x_ref[...]
    xc = _dot(x, cc_ref[...])
    xs = _dot(x, sc_ref[...])
    y = (_dot(cl_ref[...], xc) - _dot(sl_ref[...], xs)) * norm
    o_ref[...] = _dot(y, w_ref[...])


def _fourier_call(fx, w_fourier, batch, seq_len):
    t = batch * seq_len
    tm = min(DFT_ROW_TILE, seq_len)
    nmb = seq_len // tm
    j = np.arange(seq_len, dtype=np.float64)
    ang_l = 2.0 * np.pi * ((j[:, None] * j[None, :]) % seq_len) / seq_len
    hw = GROUP_WIDTH // FOURIER_HEADS
    c = np.arange(hw, dtype=np.float64)
    ang_c = 2.0 * np.pi * ((c[:, None] * c[None, :]) % hw) / hw
    eye = np.eye(FOURIER_HEADS)
    cl = jnp.asarray(np.cos(ang_l), F32)
    sl = jnp.asarray(np.sin(ang_l), F32)
    cc = jnp.asarray(np.kron(eye, np.cos(ang_c)), F32)
    sc = jnp.asarray(np.kron(eye, np.sin(ang_c)), F32)
    const = lambda shape: pl.BlockSpec(shape, lambda i, b: (0, 0))
    return pl.pallas_call(
        functools.partial(_fourier_kernel, norm=float((seq_len * hw) ** -0.5)),
        out_shape=jax.ShapeDtypeStruct((t, 256), F32),
        grid=(nmb, batch),
        in_specs=[
            pl.BlockSpec((seq_len, 256), lambda i, b: (b, 0)),
            pl.BlockSpec((tm, seq_len), lambda i, b: (i, 0)),
            pl.BlockSpec((tm, seq_len), lambda i, b: (i, 0)),
            const((256, 256)), const((256, 256)), const((256, 256)),
        ],
        out_specs=pl.BlockSpec((tm, 256), lambda i, b: (b * nmb + i, 0)),
        compiler_params=_params("parallel", "arbitrary"),
        name="fourier_mix",
    )(fx, cl, sl, cc, sc, w_fourier)


POOL_PAD = 8


def _pool_kernel(x_ref, w_ref, s_ref, o_ref, pad_ref, *, seq_len):
    x = x_ref[...]
    zeros = jnp.zeros((POOL_PAD, GROUP_WIDTH), F32)
    pad_ref[0:POOL_PAD, :] = zeros
    pad_ref[POOL_PAD + seq_len:2 * POOL_PAD + seq_len, :] = zeros
    pad_ref[POOL_PAD:POOL_PAD + seq_len, :] = x

    def shifted(off):
        return pad_ref[POOL_PAD + off:POOL_PAD + off + seq_len, :]

    t = lax.broadcasted_iota(jnp.int32, (seq_len, GROUP_WIDTH), 0)
    lane_group = lax.broadcasted_iota(jnp.int32, (seq_len, GROUP_WIDTH), 1) // POOL_GROUP_W
    acc = x
    done = 0
    pooled = jnp.zeros_like(x)
    for g, w in enumerate(POOL_WINDOWS):
        half = w // 2
        for off in list(range(-half, -done)) + list(range(max(done, 1), half)):
            acc = acc + shifted(off)
        done = half
        cnt = jnp.minimum(t + half, seq_len) - jnp.maximum(t - half, 0)
        pooled = jnp.where(lane_group == g, acc / cnt.astype(F32), pooled)
    o_ref[...] = _dot(pooled - x, w_ref[...]) * s_ref[...]


def _pool_call(px, w_pool_bd, pool_scale, batch, seq_len):
    t = batch * seq_len
    return pl.pallas_call(
        functools.partial(_pool_kernel, seq_len=seq_len),
        out_shape=jax.ShapeDtypeStruct((t, 256), F32),
        grid=(batch,),
        in_specs=[
            pl.BlockSpec((seq_len, 256), lambda b: (b, 0)),
            pl.BlockSpec((256, 256), lambda b: (0, 0)),
            pl.BlockSpec((1, 256), lambda b: (0, 0)),
        ],
        out_specs=pl.BlockSpec((seq_len, 256), lambda b: (b, 0)),
        scratch_shapes=[pltpu.VMEM((seq_len + 2 * POOL_PAD, GROUP_WIDTH), F32)],
        compiler_params=_params("parallel"),
        name="multiscale_pool",
    )(px, w_pool_bd, pool_scale)


def _outproj_kernel(a_ref, f_ref, n_ref, p_ref, x_ref, ada_ref, w_ref, g_ref, b_ref, o_ref):
    mix = (_dot(a_ref[...], w_ref[0:256, :]) + _dot(f_ref[...], w_ref[256:512, :])
           + _dot(n_ref[...], w_ref[512:768, :]) + _dot(p_ref[...], w_ref[768:1024, :]))
    gate = ada_ref[:, 2 * D_MODEL:3 * D_MODEL]
    z = ALPHA * x_ref[...] + gate * mix
    o_ref[...] = _layer_norm(z) * g_ref[...] + b_ref[...]


def _outproj_call(a_out, f_out, n_out, p_out, x, ada, cond_base, cond_span, w_out, ln_g, ln_b):
    t = x.shape[0]
    tm = TOKEN_TILE
    tiles_per_cond = cond_span // tm
    part = pl.BlockSpec((tm, 256), lambda i: (i, 0))
    row = pl.BlockSpec((1, D_MODEL), lambda i: (0, 0))
    return pl.pallas_call(
        _outproj_kernel,
        out_shape=jax.ShapeDtypeStruct((t, D_MODEL), F32),
        grid=(t // tm,),
        in_specs=[part, part, part, part,
                  pl.BlockSpec((tm, D_MODEL), lambda i: (i, 0)),
                  pl.BlockSpec((None, 1, ADA_CHUNKS * D_MODEL),
                               lambda i: (cond_base + i // tiles_per_cond, 0, 0)),
                  pl.BlockSpec((D_MODEL, D_MODEL), lambda i: (0, 0)), row, row],
        out_specs=pl.BlockSpec((tm, D_MODEL), lambda i: (i, 0)),
        compiler_params=_params("parallel"),
        name="outproj_postnorm",
    )(a_out, f_out, n_out, p_out, x, ada, w_out, ln_g, ln_b)


def _sorting_network(n):
    pairs = []
    p = 1
    while p < n:
        k = p
        while k >= 1:
            for j in range(k % p, n - k, 2 * k):
                for i in range(min(k, n - j - k)):
                    if (i + j) // (2 * p) == (i + j + k) // (2 * p):
                        pairs.append((i + j, i + j + k))
            k //= 2
        p *= 2
    return pairs


def _sort_descending(rows):
    rows = list(rows)
    for a, b in _sorting_network(len(rows)):
        rows[a], rows[b] = jnp.maximum(rows[a], rows[b]), jnp.minimum(rows[a], rows[b])
    return rows


def _sublane_max_all(x):
    x = jnp.maximum(x, pltpu.roll(x, 4, axis=0))
    x = jnp.maximum(x, pltpu.roll(x, 2, axis=0))
    return jnp.maximum(x, pltpu.roll(x, 1, axis=0))


def _extract_top(groups, count):
    groups = [list(g) for g in groups]
    out = []
    for it in range(count):
        head = groups[0][0]
        for g in groups[1:]:
            head = jnp.maximum(head, g[0])
        m = _sublane_max_all(head)
        out.append(m)
        remaining = count - it - 1
        if remaining == 0:
            break
        for g in groups:
            eq = g[0] == m
            keep = min(len(g), remaining)
            for i in range(keep):
                nxt = g[i + 1] if i + 1 < len(g) else jnp.full_like(m, -jnp.inf)
                g[i] = jnp.where(eq, nxt, g[i])
            del g[keep:]
    return out


def _peer_route(s1, s2):
    groups = PEER_N_KEYS // 8
    s1_rows = [s1[8 * i:8 * (i + 1)] for i in range(groups)]
    s2_rows = [s2[8 * i:8 * (i + 1)] for i in range(groups)]
    v1 = _extract_top([_sort_descending(s1_rows)], PEER_TOPK + 1)
    v2 = _extract_top([_sort_descending(s2_rows)], PEER_TOPK + 1)
    sub = lax.broadcasted_iota(jnp.int32, v1[0].shape, 0)
    lo = v1[0]
    hi = v1[8]
    for r in range(1, 8):
        lo = jnp.where(sub == r, v1[r], lo)
        hi = jnp.where(sub == r, v1[8 + r], hi)
    cand = [[lo + v2[b] for b in range(PEER_TOPK + 1)], [hi + v2[0]], [v1[PEER_TOPK] + v2[0]]]
    top = _extract_top(cand, PEER_TOPK + 1)
    den = jnp.ones_like(top[0])
    for tk in top[1:PEER_TOPK]:
        den = den + jnp.exp(tk - top[0])
    half_inv_den = 0.5 / den
    tau = 0.5 * (top[PEER_TOPK - 1] + top[PEER_TOPK])
    thr = [tau - r for r in s1_rows]
    coef = [jnp.exp(r - v1[0]) * half_inv_den for r in s1_rows]
    e2 = [jnp.exp(r - v2[0]) for r in s2_rows]
    return thr, coef, e2


def _peer_kernel(x_ref, ada_ref, wq_ref, k1_ref, k2_ref, u_ref, vt_ref, g_ref, b_ref, o_ref,
                 h_ref, ht_ref, yt_ref, act_ref, s2_ref, e2_ref, thr_ref, coef_ref):
    j = pl.program_id(1)
    tt = x_ref.shape[0]
    half = PEER_KEY_DIM // 2
    lanes = PEER_LANE_CHUNK

    @pl.when(j == 0)
    def _():
        sh = ada_ref[:, 3 * D_MODEL:4 * D_MODEL]
        sc = ada_ref[:, 4 * D_MODEL:5 * D_MODEL]
        h = _layer_norm(x_ref[...]) * (1.0 + sc) + sh
        h_ref[...] = h.astype(BF16)
        ht_ref[...] = h.T.astype(BF16)
        yt_ref[...] = jnp.zeros_like(yt_ref)

        def one_head(hd):
            w_rows = wq_ref[pl.ds(pl.multiple_of(hd * PEER_KEY_DIM, PEER_KEY_DIM), PEER_KEY_DIM), :]
            q_t = lax.dot_general(w_rows, h_ref[...], (((1,), (1,)), ((), ())),
                                  preferred_element_type=F32)
            s1 = _dot(k1_ref[...], q_t[0:half])
            s2 = _dot(k2_ref[...], q_t[half:2 * half])
            s2_ref[hd] = s2
            for c in range(tt // lanes):
                cols = slice(c * lanes, (c + 1) * lanes)
                thr, coef, e2 = _peer_route(s1[:, cols], s2[:, cols])
                for i in range(PEER_N_KEYS // 8):
                    rows = pl.ds(8 * i, 8)
                    thr_ref[hd, rows, cols] = thr[i]
                    coef_ref[hd, rows, cols] = coef[i]
                    e2_ref[hd, rows, cols] = e2[i]

        def head_group(g, carry):
            for k in range(PEER_HEADS_PER_ITER):
                one_head(g * PEER_HEADS_PER_ITER + k)
            return carry

        lax.fori_loop(0, PEER_HEADS // PEER_HEADS_PER_ITER, head_group, 0)

    keys_per_block = PEER_EXPERT_TILE // PEER_N_KEYS
    keys_per_sub = PEER_SUB_TILE // PEER_N_KEYS
    n_sub = PEER_EXPERT_TILE // PEER_SUB_TILE

    def up_proj(sub):
        return jnp.dot(u_ref[sub * PEER_SUB_TILE:(sub + 1) * PEER_SUB_TILE, :], ht_ref[...],
                       preferred_element_type=F32)

    acc = None
    a_next = up_proj(0)
    for sub in range(n_sub):
        sub_rows = slice(sub * PEER_SUB_TILE, (sub + 1) * PEER_SUB_TILE)
        a_sub = a_next
        if sub + 1 < n_sub:
            a_next = up_proj(sub + 1)
        for ii in range(keys_per_sub):
            i1 = j * keys_per_block + sub * keys_per_sub + ii
            rows = slice(ii * PEER_N_KEYS, (ii + 1) * PEER_N_KEYS)
            thr_rows = [thr_ref[hd, pl.ds(i1, 1), :] for hd in range(PEER_HEADS)]
            coef_rows = [coef_ref[hd, pl.ds(i1, 1), :] for hd in range(PEER_HEADS)]
            for c in range(tt // lanes):
                cols = slice(c * lanes, (c + 1) * lanes)
                w = None
                for hd in range(PEER_HEADS):
                    thr = thr_rows[hd][:, cols]
                    coef = coef_rows[hd][:, cols]
                    term = jnp.where(s2_ref[hd, :, cols] >= thr, e2_ref[hd, :, cols], 0.0) * coef
                    w = term if w is None else w + term
                a = a_sub[rows, cols]
                gelu2 = a * (1.0 + lax.erf(a * (2.0 ** -0.5)))
                act_ref[sub * PEER_SUB_TILE + ii * PEER_N_KEYS:
                        sub * PEER_SUB_TILE + (ii + 1) * PEER_N_KEYS, cols] = (w * gelu2).astype(BF16)
        part = jnp.dot(vt_ref[:, sub_rows], act_ref[sub_rows, :], preferred_element_type=F32)
        acc = part if acc is None else acc + part
    yt_ref[...] += acc

    @pl.when(j == pl.num_programs(1) - 1)
    def _():
        gate = ada_ref[:, 5 * D_MODEL:6 * D_MODEL]
        z = ALPHA * x_ref[...] + gate * yt_ref[...].T
        o_ref[...] = _layer_norm(z) * g_ref[...] + b_ref[...]


def _peer_call(x, ada, cond_base, cond_span, layer, wq_t, k1, k2, u, v_t, ln_g, ln_b):
    t = x.shape[0]
    tt = PEER_TOKEN_TILE
    tiles_per_cond = cond_span // tt
    eb = PEER_EXPERT_TILE
    row = pl.BlockSpec((1, D_MODEL), lambda i, j: (0, 0))
    per_head = pltpu.VMEM((PEER_HEADS, PEER_N_KEYS, tt), F32)
    return pl.pallas_call(
        _peer_kernel,
        out_shape=jax.ShapeDtypeStruct((t, D_MODEL), F32),
        grid=(t // tt, PEER_N_EXPERTS // eb),
        in_specs=[
            pl.BlockSpec((tt, D_MODEL), lambda i, j: (i, 0)),
            pl.BlockSpec((None, 1, ADA_CHUNKS * D_MODEL),
                         lambda i, j: (cond_base + i // tiles_per_cond, 0, 0)),
            pl.BlockSpec((PEER_HEADS * PEER_KEY_DIM, D_MODEL), lambda i, j: (0, 0)),
            pl.BlockSpec((PEER_N_KEYS, PEER_KEY_DIM // 2), lambda i, j: (0, 0)),
            pl.BlockSpec((PEER_N_KEYS, PEER_KEY_DIM // 2), lambda i, j: (0, 0)),
            pl.BlockSpec((None, eb, D_MODEL), lambda i, j: (layer, j, 0)),
            pl.BlockSpec((None, D_MODEL, eb), lambda i, j: (layer, 0, j)),
            row, row,
        ],
        out_specs=pl.BlockSpec((tt, D_MODEL), lambda i, j: (i, 0)),
        scratch_shapes=[
            pltpu.VMEM((tt, D_MODEL), BF16),
            pltpu.VMEM((D_MODEL, tt), BF16),
            pltpu.VMEM((D_MODEL, tt), F32),
            pltpu.VMEM((eb, tt), BF16),
            per_head, per_head, per_head, per_head,
        ],
        compiler_params=_params("parallel", "arbitrary"),
        name="peer",
    )(x, ada, wq_t, k1, k2, u, v_t, ln_g, ln_b)


def _rope_tables(seq_len):
    t = np.arange(seq_len)
    row = (t // GRID_W).astype(np.float64)
    col = (t % GRID_W).astype(np.float64)
    nf = HEAD_DIM // 4
    inv = ROPE_THETA ** (-np.arange(nf, dtype=np.float64) / nf)
    d = np.arange(HEAD_DIM)
    pos = np.where(d[None, :] < HEAD_DIM // 2, row[:, None], col[:, None])
    ang = pos * inv[d % nf][None, :]
    sign = np.where((d % (HEAD_DIM // 2)) < nf, -1.0, 1.0)
    cos = np.tile(np.cos(ang), (1, ATTN_HEADS))
    sin = np.tile(np.sin(ang) * sign[None, :], (1, ATTN_HEADS))
    return jnp.asarray(cos, F32), jnp.asarray(sin, F32)


def _head_mean_matrix():
    m = np.kron(np.eye(ATTN_HEADS), np.full((HEAD_DIM, HEAD_DIM), 1.0 / HEAD_DIM))
    return jnp.asarray(m, BF16)


def _block_diag(w):
    g, a, b = w.shape
    out = jnp.zeros((g * a, g * b), w.dtype)
    for i in range(g):
        out = out.at[i * a:(i + 1) * a, i * b:(i + 1) * b].set(w[i])
    return out


def _trunk_layer(x, ada, cond_base, batch, seq_len, lw, ctx_cache):
    rope = ctx_cache is not None
    if rope:
        cos_t, sin_t = _rope_tables(seq_len)
    else:
        cos_t = jnp.ones((TOKEN_TILE, 256), F32)
        sin_t = jnp.zeros((TOKEN_TILE, 256), F32)
    cond_span = seq_len if rope else batch * seq_len
    aq, ak, av, fx, nq, nk, nv, px = _inproj_call(
        x, ada, cond_base, cond_span, seq_len, lw["w_in"], lw["gq"], lw["gk"], lw["head_mean"],
        cos_t, sin_t, rope)
    if ctx_cache is None:
        a_out, n_out = _ctx_attn_call(aq, ak, av, nq, nk, nv, batch, seq_len)
    else:
        ck_a, cv_a, ck_n, cv_n = ctx_cache
        a_out = _lat_attn_call(aq, ak, av, ck_a, cv_a, batch, seq_len)
        n_out = _na_call(nq, nk, nv, ck_n, cv_n, lw["na_bias"], batch, seq_len)
    f_out = _fourier_call(fx, lw["w_fourier"], batch, seq_len)
    p_out = _pool_call(px, lw["w_pool"], lw["pool_scale"], batch, seq_len)
    x = _outproj_call(a_out, f_out, n_out, p_out, x, ada, cond_base, cond_span,
                      lw["w_out"], lw["ln1_g"], lw["ln1_b"])
    x = _peer_call(x, ada, cond_base, cond_span, lw["layer"], lw["wq_t"], lw["k1"], lw["k2"],
                   lw["u"], lw["v_t"], lw["ln2_g"], lw["ln2_b"])
    return x, (ak, av, nk, nv)


def kernel(x_prompt, x_sample, cache_attn_k, cache_attn_v, cache_na_k, cache_na_v, c, c_ctx,
           w_ada, b_ada, w_in, q_norm_g, k_norm_g, w_fourier, na_bias, w_pool, pool_scale,
           w_out, ln1_g, ln1_b, ln2_g, ln2_b, peer_wq, peer_k1, peer_k2, peer_u, peer_v):
    batch, seq, d = x_prompt.shape
    dec_batch, dec_seq, _ = x_sample.shape
    xp = x_prompt.reshape(batch * seq, d)
    xs = x_sample.reshape(dec_batch * dec_seq, d)

    cond = jnp.zeros((COND_ROWS, d), F32).at[0].set(c_ctx).at[1:1 + dec_batch].set(c)
    ada_all = _ada_call(cond, w_ada, b_ada)
    head_mean = _head_mean_matrix()
    rows = dec_seq // GRID_W

    def heads_first(cache):
        return cache.transpose(1, 0, 3, 2, 4)

    cak, cav = heads_first(cache_attn_k), heads_first(cache_attn_v)
    cnk, cnv = heads_first(cache_na_k), heads_first(cache_na_v)

    u_all = peer_u.astype(BF16)
    v_t_all = jnp.swapaxes(peer_v.astype(BF16), 1, 2)

    new_kv = [[], [], [], []]
    for l in range(DEPTH):
        lw = {
            "layer": l,
            "w_in": w_in[l].astype(BF16),
            "gq": jnp.tile(q_norm_g[l], ATTN_HEADS).reshape(1, -1),
            "gk": jnp.tile(k_norm_g[l], ATTN_KV_HEADS).reshape(1, -1),
            "head_mean": head_mean,
            "w_fourier": w_fourier[l].astype(BF16),
            "na_bias": _na_bias_patterns(na_bias[l], rows),
            "w_pool": _block_diag(w_pool[l]).astype(BF16),
            "pool_scale": pool_scale[l].reshape(1, -1),
            "w_out": w_out[l].astype(BF16),
            "ln1_g": ln1_g[l].reshape(1, -1), "ln1_b": ln1_b[l].reshape(1, -1),
            "ln2_g": ln2_g[l].reshape(1, -1), "ln2_b": ln2_b[l].reshape(1, -1),
            "wq_t": peer_wq[l].T.astype(BF16),
            "k1": peer_k1[l].astype(BF16), "k2": peer_k2[l].astype(BF16),
            "u": u_all,
            "v_t": v_t_all,
        }
        ada = ada_all[l].reshape(COND_ROWS, 1, ADA_CHUNKS * d)
        xp, kv = _trunk_layer(xp, ada, 0, batch, seq, lw, None)
        for dst, arr in zip(new_kv, kv):
            heads = arr.shape[0]
            dst.append(arr.reshape(heads, batch, seq, HEAD_DIM).transpose(1, 2, 0, 3))
        xs, _ = _trunk_layer(xs, ada, 1, dec_batch, dec_seq, lw,
                             (cak[l], cav[l], cnk[l], cnv[l]))
    outs = [jnp.stack(lst, axis=1) for lst in new_kv]
    return (xp.reshape(batch, seq, d), xs.reshape(dec_batch, dec_seq, d), *outs)
```
